```python
import math
import jax, jax.numpy as jnp
from jax import lax
import numpy as np

D_MODEL = 1024
BATCH = 8
SEQ = 4096
DEPTH = 1

CHUNK = 64
SGU_BLOCK = 128
Q_BLOCK = 128
E_A = D_MODEL
A_GROUPS = 8
A_GROUP_DIM = E_A // A_GROUPS
FOX_HEADS = 8
FOX_HEAD_DIM = D_MODEL // FOX_HEADS
E_B = FOX_HEADS * FOX_HEAD_DIM
EPS = 1e-6
NEG_INF = -1e30
SPLITS = (E_A, E_A, E_A, E_B, E_B, E_B, E_B, FOX_HEADS, D_MODEL, D_MODEL)
N_IN = sum(SPLITS)

kernel_name = "hybrid_gmlp_fox_gated_block"


def rmsnorm(x, g):
    xf = x.astype(jnp.float32)
    y = xf * lax.rsqrt(jnp.mean(xf * xf, axis=-1, keepdims=True) + EPS)
    return (y * g.astype(jnp.float32)).astype(x.dtype)


def sgu_branch(u, v, z, ln_g, ln_b, w_s, b_s):
    B, S, _ = v.shape
    nb = S // SGU_BLOCK
    vg = v.reshape(B, nb, SGU_BLOCK, A_GROUPS, A_GROUP_DIM).astype(jnp.float32)
    mu = jnp.mean(vg, axis=-1, keepdims=True)
    var = jnp.mean(jnp.square(vg - mu), axis=-1, keepdims=True)
    vn = (vg - mu) * lax.rsqrt(var + EPS)
    vn = (vn * ln_g.reshape(A_GROUPS, A_GROUP_DIM) + ln_b.reshape(A_GROUPS, A_GROUP_DIM)).astype(v.dtype)
    pos = jnp.arange(SGU_BLOCK)
    mask = (pos[:, None] // CHUNK) >= (pos[None, :] // CHUNK)
    ws = jnp.where(mask[None], w_s, jnp.zeros_like(w_s))
    y = jnp.einsum('gts,bnsgc->bntgc', ws, vn) + b_s.T[None, None, :, :, None]
    y = y.reshape(B, S, E_A)
    return u * y * jax.nn.silu(z)


def fox_branch(q, k, v, z, f_logit, b_f):
    B, S, _ = q.shape
    to_heads = lambda t: t.reshape(B, S, FOX_HEADS, FOX_HEAD_DIM).transpose(0, 2, 1, 3)
    qh, kh, vh = to_heads(q), to_heads(k), to_heads(v)
    log_f = jax.nn.log_sigmoid((f_logit + b_f).astype(jnp.float32))
    c = jnp.cumsum(log_f.transpose(0, 2, 1), axis=-1)
    scale = 1.0 / math.sqrt(FOX_HEAD_DIM)
    outs = []
    for i in range(S // Q_BLOCK):
        s0 = i * Q_BLOCK
        end = s0 + Q_BLOCK
        qb, kb, vb = qh[:, :, s0:end], kh[:, :, :end], vh[:, :, :end]
        logits = jnp.einsum('bhqd,bhkd->bhqk', qb, kb).astype(jnp.float32) * scale
        logits = logits + c[:, :, s0:end, None] - c[:, :, None, :end]
        mask = (s0 + jnp.arange(Q_BLOCK))[:, None] >= jnp.arange(end)[None, :]
        logits = jnp.where(mask[None, None], logits, NEG_INF)
        p = jax.nn.softmax(logits, axis=-1).astype(vb.dtype)
        outs.append(jnp.einsum('bhqk,bhkd->bhqd', p, vb))
    o = jnp.concatenate(outs, axis=2).transpose(0, 2, 1, 3).reshape(B, S, E_B)
    return o * jax.nn.silu(z)


def setup_inputs(seed: int = 0) -> dict:
    key = jax.random.key(seed)
    ks = jax.random.split(key, 16)
    nrm = lambda k, shape, s: jax.random.normal(k, shape, jnp.float32) * s
    return {
        "x": nrm(ks[0], (BATCH, SEQ, D_MODEL), 1.0),
        "norm1_g": 1.0 + nrm(ks[1], (D_MODEL,), 0.02),
        "w_in": nrm(ks[2], (D_MODEL, N_IN), D_MODEL ** -0.5),
        "sgu_ln_g": 1.0 + nrm(ks[3], (E_A,), 0.02),
        "sgu_ln_b": nrm(ks[4], (E_A,), 0.02),
        "w_spatial": nrm(ks[5], (A_GROUPS, SGU_BLOCK, SGU_BLOCK), SGU_BLOCK ** -0.5),
        "b_spatial": 1.0 + nrm(ks[6], (A_GROUPS, SGU_BLOCK), 0.02),
        "b_forget": 3.0 + nrm(ks[7], (FOX_HEADS,), 0.5),
        "b_gate": nrm(ks[8], (2, D_MODEL), 0.02),
        "w_proj_a": nrm(ks[9], (E_A, D_MODEL), E_A ** -0.5),
        "w_proj_b": nrm(ks[10], (E_B, D_MODEL), E_B ** -0.5),
        "w_out": nrm(ks[11], (D_MODEL, D_MODEL), D_MODEL ** -0.5),
        "norm_f_g": 1.0 + nrm(ks[12], (D_MODEL,), 0.02),
    }


def reference(x, norm1_g, w_in, sgu_ln_g, sgu_ln_b, w_spatial, b_spatial, b_forget,
              b_gate, w_proj_a, w_proj_b, w_out, norm_f_g):
    h = x
    for _ in range(DEPTH):
        xn = rmsnorm(h, norm1_g)
        proj = jnp.einsum('bsd,dn->bsn', xn, w_in)
        idx = list(np.cumsum(SPLITS)[:-1])
        u_a, v_a, z_a, q, k, v, z_b, f_logit, g_a, g_b = jnp.split(proj, idx, axis=-1)
        a = sgu_branch(u_a, v_a, z_a, sgu_ln_g, sgu_ln_b, w_spatial, b_spatial)
        o = fox_branch(q, k, v, z_b, f_logit, b_forget)
        merged = (jax.nn.sigmoid(g_a + b_gate[0]) * jnp.einsum('bse,ed->bsd', a, w_proj_a)
                  + jax.nn.sigmoid(g_b + b_gate[1]) * jnp.einsum('bse,ed->bsd', o, w_proj_b))
        h = h + jnp.einsum('bsd,de->bse', merged, w_out)
    return rmsnorm(h, norm_f_g)
```

```python
import functools
import math

import jax
import jax.numpy as jnp
from jax import lax
from jax.experimental import pallas as pl
from jax.experimental.pallas import tpu as pltpu

D_MODEL = 1024
CHUNK = 64
SGU_BLOCK = 128
A_GROUPS = 8
A_GROUP_DIM = D_MODEL // A_GROUPS
FOX_HEADS = 8
FOX_HEAD_DIM = D_MODEL // FOX_HEADS
EPS = 1e-6
NEG_INF = -1e30
LOG2E = math.log2(math.e)

LANES = 128
TS = 256
TQ = 256
TK = 256
VMEM_LIMIT_BYTES = 56 * 1024 * 1024

F32 = jnp.float32
BF16 = jnp.bfloat16


def _dot(a, b):
    return jnp.dot(a, b, preferred_element_type=F32)


def _sigmoid(t):
    return 1.0 / (1.0 + jnp.exp(-t))


def _proj_kernel(x_ref, g1_ref, wsgu_ref, wqkvz_ref, wf_ref, wg_ref, lng_ref, lnb_ref, ws_ref,
                 bst_ref, bf_ref, bg_ref, wpa_ref,
                 q_ref, k_ref, v_ref, zb_ref, gb_ref, ma_ref, ccol_ref, crow_ref,
                 a_scr, carry_scr):
    @pl.when(pl.program_id(1) == 0)
    def _():
        carry_scr[...] = jnp.zeros_like(carry_scr)

    x = x_ref[0]
    ms = jnp.mean(x * x, axis=-1, keepdims=True)
    xn = (x * lax.rsqrt(ms + EPS) * g1_ref[...]).astype(BF16)

    uvz = _dot(xn, wsgu_ref[...])
    pos = lax.broadcasted_iota(jnp.int32, (SGU_BLOCK, SGU_BLOCK), 0) // CHUNK
    src = lax.broadcasted_iota(jnp.int32, (SGU_BLOCK, SGU_BLOCK), 1) // CHUNK
    causal = pos >= src
    for g in range(A_GROUPS):
        c0 = g * A_GROUP_DIM
        ws = jnp.where(causal, ws_ref[g], 0.0).astype(BF16)
        bias = bst_ref[:, g:g + 1]
        lng = lng_ref[:, c0:c0 + A_GROUP_DIM]
        lnb = lnb_ref[:, c0:c0 + A_GROUP_DIM]
        for n in range(TS // SGU_BLOCK):
            r0 = n * SGU_BLOCK
            u = uvz[r0:r0 + SGU_BLOCK, c0:c0 + A_GROUP_DIM]
            v = uvz[r0:r0 + SGU_BLOCK, D_MODEL + c0:D_MODEL + c0 + A_GROUP_DIM]
            z = uvz[r0:r0 + SGU_BLOCK, 2 * D_MODEL + c0:2 * D_MODEL + c0 + A_GROUP_DIM]
            mu = jnp.mean(v, axis=-1, keepdims=True)
            d = v - mu
            var = jnp.mean(d * d, axis=-1, keepdims=True)
            vn = ((d * lax.rsqrt(var + EPS)) * lng + lnb).astype(BF16)
            y = _dot(ws, vn) + bias
            a = u * y * (z * _sigmoid(z))
            a_scr[r0:r0 + SGU_BLOCK, c0:c0 + A_GROUP_DIM] = a.astype(BF16)

    gates = _dot(xn, wg_ref[...]) + bg_ref[...]
    ma = _sigmoid(gates[:, :D_MODEL]) * _dot(a_scr[...], wpa_ref[...])
    ma_ref[0] = ma.astype(BF16)
    gb_ref[0] = _sigmoid(gates[:, D_MODEL:]).astype(BF16)

    qkvz = _dot(xn, wqkvz_ref[...])
    q_ref[0] = (qkvz[:, :D_MODEL] * (LOG2E / math.sqrt(FOX_HEAD_DIM))).astype(BF16)
    k_ref[0] = qkvz[:, D_MODEL:2 * D_MODEL].astype(BF16)
    v_ref[0] = qkvz[:, 2 * D_MODEL:3 * D_MODEL].astype(BF16)
    zb = qkvz[:, 3 * D_MODEL:]
    zb_ref[0] = (zb * _sigmoid(zb)).astype(BF16)

    f = _dot(xn, wf_ref[...]) + bf_ref[...]
    logf = jnp.minimum(f, 0.0) - jnp.log1p(jnp.exp(-jnp.abs(f)))
    tri = (lax.broadcasted_iota(jnp.int32, (TS, TS), 0)
           >= lax.broadcasted_iota(jnp.int32, (TS, TS), 1)).astype(BF16)
    hi = logf.astype(BF16)
    r1 = logf - hi.astype(F32)
    mid = r1.astype(BF16)
    lo = (r1 - mid.astype(F32)).astype(BF16)
    c = carry_scr[...] + (_dot(tri, hi) + _dot(tri, mid) + _dot(tri, lo))
    carry_scr[...] = c[TS - 1:TS, :]
    c2 = c * LOG2E
    ccol_ref[0] = c2[:, :FOX_HEADS]
    crow_ref[0, 0] = c2.T[:FOX_HEADS, :]


def _attn_kernel(q_ref, k_ref, v_ref, crow_ref, ccol_ref, zb_ref, gb_ref, ma_ref, x_ref,
                 wpb_ref, wout_ref, gf_ref, out_ref, o_scr):
    qi = pl.program_id(1)
    row = lax.broadcasted_iota(jnp.int32, (TQ, TK), 0)
    col = lax.broadcasted_iota(jnp.int32, (TQ, TK), 1)
    diag_mask = row >= col

    for h in range(FOX_HEADS):
        c0 = h * FOX_HEAD_DIM
        q = q_ref[0, :, c0:c0 + FOX_HEAD_DIM]
        cq = ccol_ref[0, :, h:h + 1]

        def scores(j):
            k0 = pl.multiple_of(j * TK, TK)
            kb = k_ref[0, pl.ds(k0, TK), c0:c0 + FOX_HEAD_DIM]
            s = lax.dot_general(q, kb, (((1,), (1,)), ((), ())), preferred_element_type=F32)
            ck = crow_ref[0, j, h:h + 1, :]
            return s + cq - ck, k0

        def update(s, k0, carry):
            m, l, acc = carry
            m_new = jnp.maximum(m, jnp.max(s, axis=-1, keepdims=True))
            alpha = jnp.exp2(m - m_new)
            p = jnp.exp2(s - m_new)
            l = alpha * l + jnp.sum(p, axis=-1, keepdims=True)
            vb = v_ref[0, pl.ds(k0, TK), c0:c0 + FOX_HEAD_DIM]
            acc = alpha * acc + _dot(p.astype(BF16), vb)
            return m_new, l, acc

        def body(j, carry):
            s, k0 = scores(j)
            return update(s, k0, carry)

        init = (jnp.full((TQ, 1), NEG_INF, F32), jnp.zeros((TQ, 1), F32),
                jnp.zeros((TQ, FOX_HEAD_DIM), F32))
        carry = lax.fori_loop(0, qi, body, init)
        s, k0 = scores(qi)
        m, l, acc = update(jnp.where(diag_mask, s, NEG_INF), k0, carry)
        o = (acc / l) * zb_ref[0, :, c0:c0 + FOX_HEAD_DIM].astype(F32)
        o_scr[:, c0:c0 + FOX_HEAD_DIM] = o.astype(BF16)

    pb = _dot(o_scr[...], wpb_ref[...])
    merged = gb_ref[0].astype(F32) * pb + ma_ref[0].astype(F32)
    hres = x_ref[0] + _dot(merged.astype(BF16), wout_ref[...])
    ms = jnp.mean(hres * hres, axis=-1, keepdims=True)
    out_ref[0] = hres * lax.rsqrt(ms + EPS) * gf_ref[...]


def _resident(shape):
    return pl.BlockSpec(shape, lambda b, s: (0,) * len(shape), pipeline_mode=pl.Buffered(1))


def kernel(x, norm1_g, w_in, sgu_ln_g, sgu_ln_b, w_spatial, b_spatial, b_forget, b_gate,
           w_proj_a, w_proj_b, w_out, norm_f_g):
    B, S, D = x.shape
    assert D == D_MODEL and S % TS == 0 and S % TQ == 0 and TQ == TK and TS == TK
    H = FOX_HEADS

    w_bf = w_in.astype(BF16)
    w_sgu = w_bf[:, :3 * D]
    w_qkvz = w_bf[:, 3 * D:7 * D]
    w_f = jnp.pad(w_bf[:, 7 * D:7 * D + H], ((0, 0), (0, LANES - H)))
    w_g = w_bf[:, 7 * D + H:]
    b_f = jnp.pad(b_forget, (0, LANES - H)).reshape(1, LANES)
    row = lambda t: t.reshape(1, -1)

    tok = lambda: pl.BlockSpec((1, TS, D), lambda b, s: (b, s, 0))
    tok_out = jax.ShapeDtypeStruct((B, S, D), BF16)
    q, k, v, zb, gb, ma, ccol, crow = pl.pallas_call(
        _proj_kernel,
        grid=(B, S // TS),
        in_specs=[
            tok(),
            _resident((1, D)), _resident((D, 3 * D)), _resident((D, 4 * D)), _resident((D, LANES)),
            _resident((D, 2 * D)), _resident((1, D)), _resident((1, D)),
            _resident((A_GROUPS, SGU_BLOCK, SGU_BLOCK)), _resident((SGU_BLOCK, A_GROUPS)),
            _resident((1, LANES)), _resident((1, 2 * D)), _resident((D, D)),
        ],
        out_specs=[tok(), tok(), tok(), tok(), tok(), tok(),
                   pl.BlockSpec((1, TS, H), lambda b, s: (b, s, 0)),
                   pl.BlockSpec((1, 1, H, TS), lambda b, s: (b, s, 0, 0))],
        out_shape=[tok_out] * 6 + [jax.ShapeDtypeStruct((B, S, H), F32),
                                   jax.ShapeDtypeStruct((B, S // TS, H, TS), F32)],
        scratch_shapes=[pltpu.VMEM((TS, D), BF16), pltpu.VMEM((1, LANES), F32)],
        compiler_params=pltpu.CompilerParams(
            dimension_semantics=("arbitrary", "arbitrary"), vmem_limit_bytes=VMEM_LIMIT_BYTES),
        name="proj_sgu",
    )(x, row(norm1_g), w_sgu, w_qkvz, w_f, w_g, row(sgu_ln_g), row(sgu_ln_b), w_spatial,
      b_spatial.T, b_f, b_gate.reshape(1, 2 * D), w_proj_a.astype(BF16))

    qtok = lambda: pl.BlockSpec((1, TQ, D), lambda b, s: (b, s, 0))
    seq = lambda: pl.BlockSpec((1, S, D), lambda b, s: (b, 0, 0), pipeline_mode=pl.Buffered(1))
    return pl.pallas_call(
        _attn_kernel,
        grid=(B, S // TQ),
        in_specs=[
            qtok(), seq(), seq(),
            pl.BlockSpec((1, S // TK, H, TK), lambda b, s: (b, 0, 0, 0)),
            pl.BlockSpec((1, TQ, H), lambda b, s: (b, s, 0)),
            qtok(), qtok(), qtok(), qtok(),
            _resident((D, D)), _resident((D, D)), _resident((1, D)),
        ],
        out_specs=qtok(),
        out_shape=jax.ShapeDtypeStruct((B, S, D), x.dtype),
        scratch_shapes=[pltpu.VMEM((TQ, D), BF16)],
        compiler_params=pltpu.CompilerParams(
            dimension_semantics=("arbitrary", "arbitrary"), vmem_limit_bytes=VMEM_LIMIT_BYTES),
        name="fox_merge",
    )(q, k, v, crow, ccol, zb, gb, ma, x, w_proj_b.astype(BF16), w_out.astype(BF16),
      row(norm_f_g))
```

```python
import functools
import math

import jax
import jax.numpy as jnp
from jax import lax
from jax.experimental import pallas as pl
from jax.experimental.pallas import tpu as pltpu

D_MODEL = 1024
CHUNK = 64
SGU_BLOCK = 128
A_GROUPS = 8
A_GROUP_DIM = D_MODEL // A_GROUPS
FOX_HEADS = 8
FOX_HEAD_DIM = D_MODEL // FOX_HEADS
EPS = 1e-6
NEG_INF = -1e30
LOG2E = math.log2(math.e)

LANES = 128
TS = 256
TQ = 256
TK = 256
VMEM_LIMIT_BYTES = 56 * 1024 * 1024

F32 = jnp.float32
BF16 = jnp.bfloat16


def _dot(a, b):
    return jnp.dot(a, b, preferred_element_type=F32)


def _sigmoid(t):
    return 1.0 / (1.0 + jnp.exp(-t))


def _proj_kernel(x_ref, g1_ref, wsgu_ref, wqkvz_ref, wf_ref, wg_ref, lng_ref, lnb_ref, ws_ref,
                 bst_ref, bf_ref, bg_ref, wpa_ref,
                 q_ref, k_ref, v_ref, zb_ref, gb_ref, ma_ref, ccol_ref, crow_ref,
                 a_scr, carry_scr):
    @pl.when(pl.program_id(1) == 0)
    def _():
        carry_scr[...] = jnp.zeros_like(carry_scr)

    x = x_ref[0]
    ms = jnp.mean(x * x, axis=-1, keepdims=True)
    xn = (x * lax.rsqrt(ms + EPS) * g1_ref[...]).astype(BF16)

    uvz = _dot(xn, wsgu_ref[...])
    pos = lax.broadcasted_iota(jnp.int32, (SGU_BLOCK, SGU_BLOCK), 0) // CHUNK
    src = lax.broadcasted_iota(jnp.int32, (SGU_BLOCK, SGU_BLOCK), 1) // CHUNK
    causal = pos >= src
    for g in range(A_GROUPS):
        c0 = g * A_GROUP_DIM
        ws = jnp.where(causal, ws_ref[g], 0.0).astype(BF16)
        bias = bst_ref[:, g:g + 1]
        lng = lng_ref[:, c0:c0 + A_GROUP_DIM]
        lnb = lnb_ref[:, c0:c0 + A_GROUP_DIM]
        for n in range(TS // SGU_BLOCK):
            r0 = n * SGU_BLOCK
            u = uvz[r0:r0 + SGU_BLOCK, c0:c0 + A_GROUP_DIM]
            v = uvz[r0:r0 + SGU_BLOCK, D_MODEL + c0:D_MODEL + c0 + A_GROUP_DIM]
            z = uvz[r0:r0 + SGU_BLOCK, 2 * D_MODEL + c0:2 * D_MODEL + c0 + A_GROUP_DIM]
            mu = jnp.mean(v, axis=-1, keepdims=True)
            d = v - mu
            var = jnp.mean(d * d, axis=-1, keepdims=True)
            vn = ((d * lax.rsqrt(var + EPS)) * lng + lnb).astype(BF16)
            y = _dot(ws, vn) + bias
            a = u * y * (z * _sigmoid(z))
            a_scr[r0:r0 + SGU_BLOCK, c0:c0 + A_GROUP_DIM] = a.astype(BF16)

    gates = _dot(xn, wg_ref[...]) + bg_ref[...]
    ma = _sigmoid(gates[:, :D_MODEL]) * _dot(a_scr[...], wpa_ref[...])
    ma_ref[0] = ma.astype(BF16)
    gb_ref[0] = _sigmoid(gates[:, D_MODEL:]).astype(BF16)

    qkvz = _dot(xn, wqkvz_ref[...])
    q_ref[0] = (qkvz[:, :D_MODEL] * (LOG2E / math.sqrt(FOX_HEAD_DIM))).astype(BF16)
    k_ref[0] = qkvz[:, D_MODEL:2 * D_MODEL].astype(BF16)
    v_ref[0] = qkvz[:, 2 * D_MODEL:3 * D_MODEL].astype(BF16)
    zb = qkvz[:, 3 * D_MODEL:]
    zb_ref[0] = (zb * _sigmoid(zb)).astype(BF16)

    f = _dot(xn, wf_ref[...]) + bf_ref[...]
    logf = jnp.minimum(f, 0.0) - jnp.log1p(jnp.exp(-jnp.abs(f)))
    tri = (lax.broadcasted_iota(jnp.int32, (TS, TS), 0)
           >= lax.broadcasted_iota(jnp.int32, (TS, TS), 1)).astype(BF16)
    hi = logf.astype(BF16)
    r1 = logf - hi.astype(F32)
    mid = r1.astype(BF16)
    lo = (r1 - mid.astype(F32)).astype(BF16)
    c = carry_scr[...] + (_dot(tri, hi) + _dot(tri, mid) + _dot(tri, lo))
    carry_scr[...] = c[TS - 1:TS, :]
    c2 = c * LOG2E
    ccol_ref[0] = c2[:, :FOX_HEADS]
    crow_ref[0, 0] = c2.T[:FOX_HEADS, :]


def _attn_kernel(q_ref, k_ref, v_ref, crow_ref, ccol_ref, zb_ref, gb_ref, ma_ref, x_ref,
                 wpb_ref, wout_ref, gf_ref, out_ref, o_scr, m_scr, l_scr, acc_scr, cq_scr):
    qi = pl.program_id(1)
    row = lax.broadcasted_iota(jnp.int32, (TQ, TK), 0)
    col = lax.broadcasted_iota(jnp.int32, (TQ, TK), 1)
    diag_mask = row >= col

    for h in range(FOX_HEADS):
        m_scr[h] = jnp.full((TQ, LANES), NEG_INF, F32)
        l_scr[h] = jnp.zeros((TQ, LANES), F32)
        cq_scr[h] = jnp.broadcast_to(ccol_ref[0, :, h:h + 1], (TQ, LANES))
    acc_scr[...] = jnp.zeros_like(acc_scr)

    def block(j, h, masked):
        c0 = h * FOX_HEAD_DIM
        k0 = pl.multiple_of(j * TK, TK)
        q = q_ref[0, :, c0:c0 + FOX_HEAD_DIM]
        kb = k_ref[0, pl.ds(k0, TK), c0:c0 + FOX_HEAD_DIM]
        s = lax.dot_general(q, kb, (((1,), (1,)), ((), ())), preferred_element_type=F32)
        a = s - crow_ref[0, j, h:h + 1, :]
        if masked:
            a = jnp.where(diag_mask, a, NEG_INF)
        tiles = [a[:, t * LANES:(t + 1) * LANES] for t in range(TK // LANES)]
        a_max = jnp.max(functools.reduce(jnp.maximum, tiles), axis=-1, keepdims=True)
        cq = cq_scr[h]
        m_old = m_scr[h]
        m_new = jnp.maximum(m_old, a_max + cq)
        shift = cq - m_new
        alpha = jnp.exp2(m_old - m_new)
        ps = [jnp.exp2(t + shift) for t in tiles]
        m_scr[h] = m_new
        l_scr[h] = alpha * l_scr[h] + functools.reduce(jnp.add, ps)
        p = jnp.concatenate([t.astype(BF16) for t in ps], axis=1)
        vb = v_ref[0, pl.ds(k0, TK), c0:c0 + FOX_HEAD_DIM]
        acc_scr[:, c0:c0 + FOX_HEAD_DIM] = alpha * acc_scr[:, c0:c0 + FOX_HEAD_DIM] + _dot(p, vb)

    @pl.loop(0, qi)
    def _(j):
        for h in range(FOX_HEADS):
            block(j, h, masked=False)

    for h in range(FOX_HEADS):
        block(qi, h, masked=True)

    for h in range(FOX_HEADS):
        c0 = h * FOX_HEAD_DIM
        l = jnp.sum(l_scr[h], axis=-1, keepdims=True)
        o = (acc_scr[:, c0:c0 + FOX_HEAD_DIM] / l) * zb_ref[0, :, c0:c0 + FOX_HEAD_DIM].astype(F32)
        o_scr[:, c0:c0 + FOX_HEAD_DIM] = o.astype(BF16)

    pb = _dot(o_scr[...], wpb_ref[...])
    merged = gb_ref[0].astype(F32) * pb + ma_ref[0].astype(F32)
    hres = x_ref[0] + _dot(merged.astype(BF16), wout_ref[...])
    ms = jnp.mean(hres * hres, axis=-1, keepdims=True)
    out_ref[0] = hres * lax.rsqrt(ms + EPS) * gf_ref[...]


def _resident(shape):
    return pl.BlockSpec(shape, lambda b, s: (0,) * len(shape), pipeline_mode=pl.Buffered(1))


def kernel(x, norm1_g, w_in, sgu_ln_g, sgu_ln_b, w_spatial, b_spatial, b_forget, b_gate,
           w_proj_a, w_proj_b, w_out, norm_f_g):
    B, S, D = x.shape
    assert D == D_MODEL and S % TS == 0 and S % TQ == 0 and TQ == TK and TS == TK
    H = FOX_HEADS

    w_bf = w_in.astype(BF16)
    w_sgu = w_bf[:, :3 * D]
    w_qkvz = w_bf[:, 3 * D:7 * D]
    w_f = jnp.pad(w_bf[:, 7 * D:7 * D + H], ((0, 0), (0, LANES - H)))
    w_g = w_bf[:, 7 * D + H:]
    b_f = jnp.pad(b_forget, (0, LANES - H)).reshape(1, LANES)
    row = lambda t: t.reshape(1, -1)

    tok = lambda: pl.BlockSpec((1, TS, D), lambda b, s: (b, s, 0))
    tok_out = jax.ShapeDtypeStruct((B, S, D), BF16)
    q, k, v, zb, gb, ma, ccol, crow = pl.pallas_call(
        _proj_kernel,
        grid=(B, S // TS),
        in_specs=[
            tok(),
            _resident((1, D)), _resident((D, 3 * D)), _resident((D, 4 * D)), _resident((D, LANES)),
            _resident((D, 2 * D)), _resident((1, D)), _resident((1, D)),
            _resident((A_GROUPS, SGU_BLOCK, SGU_BLOCK)), _resident((SGU_BLOCK, A_GROUPS)),
            _resident((1, LANES)), _resident((1, 2 * D)), _resident((D, D)),
        ],
        out_specs=[tok(), tok(), tok(), tok(), tok(), tok(),
                   pl.BlockSpec((1, TS, H), lambda b, s: (b, s, 0)),
                   pl.BlockSpec((1, 1, H, TS), lambda b, s: (b, s, 0, 0))],
        out_shape=[tok_out] * 6 + [jax.ShapeDtypeStruct((B, S, H), F32),
                                   jax.ShapeDtypeStruct((B, S // TS, H, TS), F32)],
        scratch_shapes=[pltpu.VMEM((TS, D), BF16), pltpu.VMEM((1, LANES), F32)],
        compiler_params=pltpu.CompilerParams(
            dimension_semantics=("arbitrary", "arbitrary"), vmem_limit_bytes=VMEM_LIMIT_BYTES),
        name="proj_sgu",
    )(x, row(norm1_g), w_sgu, w_qkvz, w_f, w_g, row(sgu_ln_g), row(sgu_ln_b), w_spatial,
      b_spatial.T, b_f, b_gate.reshape(1, 2 * D), w_proj_a.astype(BF16))

    qtok = lambda: pl.BlockSpec((1, TQ, D), lambda b, s: (b, s, 0))
    seq = lambda: pl.BlockSpec((1, S, D), lambda b, s: (b, 0, 0), pipeline_mode=pl.Buffered(1))
    return pl.pallas_call(
        _attn_kernel,
        grid=(B, S // TQ),
        in_specs=[
            qtok(), seq(), seq(),
            pl.BlockSpec((1, S // TK, H, TK), lambda b, s: (b, 0, 0, 0)),
            pl.BlockSpec((1, TQ, H), lambda b, s: (b, s, 0)),
            qtok(), qtok(), qtok(), qtok(),
            _resident((D, D)), _resident((D, D)), _resident((1, D)),
        ],
        out_specs=qtok(),
        out_shape=jax.ShapeDtypeStruct((B, S, D), x.dtype),
        scratch_shapes=[pltpu.VMEM((TQ, D), BF16), pltpu.VMEM((H, TQ, LANES), F32),
                        pltpu.VMEM((H, TQ, LANES), F32), pltpu.VMEM((TQ, D), F32),
                        pltpu.VMEM((H, TQ, LANES), F32)],
        compiler_params=pltpu.CompilerParams(
            dimension_semantics=("arbitrary", "arbitrary"), vmem_limit_bytes=VMEM_LIMIT_BYTES),
        name="fox_merge",
    )(q, k, v, crow, ccol, zb, gb, ma, x, w_proj_b.astype(BF16), w_out.astype(BF16),
      row(norm_f_g))
```

```python
import math

import numpy as np
import jax
import jax.numpy as jnp
from jax import lax
from jax.experimental import pallas as pl
from jax.experimental.pallas import tpu as pltpu

D_MODEL = 1024
CHUNK = 64
SGU_BLOCK = 128
A_GROUPS = 8
A_GROUP_DIM = D_MODEL // A_GROUPS
FOX_HEADS = 8
FOX_HEAD_DIM = D_MODEL // FOX_HEADS
EPS = 1e-6
NEG_INF = -1e30
LOG2E = math.log2(math.e)

LANES = 128
BF16_ROWS = 16
TS = 256
TQ = 256
TK = 256
VMEM_LIMIT_BYTES = 56 * 1024 * 1024
BIAS_SLOTS = LANES // FOX_HEADS
ACC_ROWS = FOX_HEAD_DIM + BF16_ROWS

F32 = jnp.float32
BF16 = jnp.bfloat16
_NT = (((1,), (1,)), ((), ()))


def _dot(a, b):
    return jnp.dot(a, b, preferred_element_type=F32)


def _dot_nt(a, b):
    return lax.dot_general(a, b, _NT, preferred_element_type=F32)


def _sigmoid(t):
    return 1.0 / (1.0 + jnp.exp(-t))


def _split3(t):
    hi = t.astype(BF16)
    r1 = t - hi.astype(F32)
    mid = r1.astype(BF16)
    lo = (r1 - mid.astype(F32)).astype(BF16)
    return hi, mid, lo


def _proj_kernel(x_ref, g1_ref, wsgu_ref, wkz_ref, wqvt_ref, wf_ref, wg_ref, lng_ref, lnb_ref,
                 ws_ref, bst_ref, bf_ref, bg_ref, wpa_ref, pk_ref, kones_ref, pq_ref, qones_ref,
                 qt_ref, qbt_ref, k_ref, kb_ref, vt_ref, zb_ref, gb_ref, ma_ref,
                 a_scr, carry_scr):
    @pl.when(pl.program_id(1) == 0)
    def _():
        carry_scr[...] = jnp.zeros_like(carry_scr)

    x = x_ref[0]
    ms = jnp.mean(x * x, axis=-1, keepdims=True)
    xn = (x * lax.rsqrt(ms + EPS) * g1_ref[...]).astype(BF16)

    uvz = _dot(xn, wsgu_ref[...])
    pos = lax.broadcasted_iota(jnp.int32, (SGU_BLOCK, SGU_BLOCK), 0) // CHUNK
    src = lax.broadcasted_iota(jnp.int32, (SGU_BLOCK, SGU_BLOCK), 1) // CHUNK
    causal = pos >= src
    for g in range(A_GROUPS):
        c0 = g * A_GROUP_DIM
        ws = jnp.where(causal, ws_ref[g], 0.0).astype(BF16)
        bias = bst_ref[:, g:g + 1]
        lng = lng_ref[:, c0:c0 + A_GROUP_DIM]
        lnb = lnb_ref[:, c0:c0 + A_GROUP_DIM]
        for n in range(TS // SGU_BLOCK):
            r0 = n * SGU_BLOCK
            u = uvz[r0:r0 + SGU_BLOCK, c0:c0 + A_GROUP_DIM]
            v = uvz[r0:r0 + SGU_BLOCK, D_MODEL + c0:D_MODEL + c0 + A_GROUP_DIM]
            z = uvz[r0:r0 + SGU_BLOCK, 2 * D_MODEL + c0:2 * D_MODEL + c0 + A_GROUP_DIM]
            mu = jnp.mean(v, axis=-1, keepdims=True)
            d = v - mu
            var = jnp.mean(d * d, axis=-1, keepdims=True)
            vn = ((d * lax.rsqrt(var + EPS)) * lng + lnb).astype(BF16)
            y = _dot(ws, vn) + bias
            a = u * y * (z * _sigmoid(z))
            a_scr[r0:r0 + SGU_BLOCK, c0:c0 + A_GROUP_DIM] = a.astype(BF16)

    gates = _dot(xn, wg_ref[...]) + bg_ref[...]
    ma = _sigmoid(gates[:, :D_MODEL]) * _dot(a_scr[...], wpa_ref[...])
    ma_ref[0] = ma.astype(BF16)
    gb_ref[0] = _sigmoid(gates[:, D_MODEL:]).astype(BF16)

    kz = _dot(xn, wkz_ref[...])
    k_ref[0] = kz[:, :D_MODEL].astype(BF16)
    zb = kz[:, D_MODEL:]
    zb_ref[0] = (zb * _sigmoid(zb)).astype(BF16)
    qvt = _dot_nt(wqvt_ref[...], xn)
    qt_ref[0, 0] = (qvt[:D_MODEL] * (LOG2E / math.sqrt(FOX_HEAD_DIM))).astype(BF16)
    vt_ref[0, 0] = qvt[D_MODEL:].astype(BF16)

    f = _dot(xn, wf_ref[...]) + bf_ref[...]
    logf = jnp.minimum(f, 0.0) - jnp.log1p(jnp.exp(-jnp.abs(f)))
    tri = (lax.broadcasted_iota(jnp.int32, (TS, TS), 0)
           >= lax.broadcasted_iota(jnp.int32, (TS, TS), 1)).astype(BF16)
    c = carry_scr[...] + sum(_dot(tri, t) for t in _split3(logf))
    carry_scr[...] = c[TS - 1:TS, :]
    parts = _split3(c * LOG2E)
    kb = kones_ref[...] + sum(_dot(t, pk_ref[i]) for i, t in enumerate(parts))
    kb_ref[0] = kb.astype(BF16)
    qb = qones_ref[...] + sum(_dot_nt(pq_ref[i], t) for i, t in enumerate(parts))
    qbt_ref[0, 0] = qb.astype(BF16)


def _attn_kernel(qt_ref, qbt_ref, k_ref, kb_ref, vt_ref, zb_ref, gb_ref, ma_ref, x_ref,
                 wpb_ref, wout_ref, gf_ref, out_ref, o_scr, qa_scr, m_scr, acc_scr):
    qi = pl.program_id(1)
    kpos = lax.broadcasted_iota(jnp.int32, (TK, TQ), 0)
    qpos = lax.broadcasted_iota(jnp.int32, (TK, TQ), 1)
    diag_mask = kpos <= qpos
    slot_head = lax.broadcasted_iota(jnp.int32, (LANES, TQ), 0) // BIAS_SLOTS
    qbt = qbt_ref[0, 0]
    for h in range(FOX_HEADS):
        c0 = h * FOX_HEAD_DIM
        qa_scr[h, :FOX_HEAD_DIM, :] = qt_ref[0, 0, c0:c0 + FOX_HEAD_DIM, :]
        qa_scr[h, FOX_HEAD_DIM:, :] = jnp.where(slot_head == h, qbt, jnp.zeros_like(qbt))
    m_scr[...] = jnp.full(m_scr.shape, NEG_INF, F32)
    acc_scr[...] = jnp.zeros_like(acc_scr)
    ones_rows = jnp.ones((BF16_ROWS, TK), BF16)

    def scores(j, h):
        c0 = h * FOX_HEAD_DIM
        k0 = pl.multiple_of(j * TK, TK)
        ka = jnp.concatenate([k_ref[0, pl.ds(k0, TK), c0:c0 + FOX_HEAD_DIM],
                              kb_ref[0, pl.ds(k0, TK), :]], axis=1)
        return _dot(ka, qa_scr[h])

    def update(j, h, st):
        c0 = h * FOX_HEAD_DIM
        m_old = m_scr[h, 0:1, :]
        m_new = jnp.maximum(m_old, jnp.max(st, axis=0, keepdims=True))
        alpha = jnp.exp2(m_old - m_new)
        pt = jnp.exp2(st - m_new).astype(BF16)
        m_scr[h, 0:1, :] = m_new
        va = jnp.concatenate([vt_ref[0, j, c0:c0 + FOX_HEAD_DIM, :], ones_rows], axis=0)
        acc_scr[h] = alpha * acc_scr[h] + _dot(va, pt)

    def key_block(j, masked):
        sts = [scores(j, h) for h in range(FOX_HEADS)]
        for h in range(FOX_HEADS):
            st = jnp.where(diag_mask, sts[h], NEG_INF) if masked else sts[h]
            update(j, h, st)

    @pl.loop(0, qi)
    def _(j):
        key_block(j, masked=False)

    key_block(qi, masked=True)

    for h in range(FOX_HEADS):
        c0 = h * FOX_HEAD_DIM
        acc = acc_scr[h]
        ot = acc[:FOX_HEAD_DIM] / acc[FOX_HEAD_DIM:FOX_HEAD_DIM + 1]
        o = ot.T * zb_ref[0, :, c0:c0 + FOX_HEAD_DIM].astype(F32)
        o_scr[:, c0:c0 + FOX_HEAD_DIM] = o.astype(BF16)

    pb = _dot(o_scr[...], wpb_ref[...])
    merged = gb_ref[0].astype(F32) * pb + ma_ref[0].astype(F32)
    hres = x_ref[0] + _dot(merged.astype(BF16), wout_ref[...])
    ms = jnp.mean(hres * hres, axis=-1, keepdims=True)
    out_ref[0] = hres * lax.rsqrt(ms + EPS) * gf_ref[...]


def _resident(shape):
    return pl.BlockSpec(shape, lambda b, s: (0,) * len(shape), pipeline_mode=pl.Buffered(1))


def _bias_slot_constants():
    pk = np.zeros((3, LANES, LANES), np.float32)
    pq = np.zeros((3, LANES, LANES), np.float32)
    kones = np.zeros((1, LANES), np.float32)
    qones = np.zeros((LANES, TS), np.float32)
    for h in range(FOX_HEADS):
        for i in range(3):
            pk[i, h, BIAS_SLOTS * h + i] = -1.0
            qones[BIAS_SLOTS * h + i, :] = 1.0
            pq[i, BIAS_SLOTS * h + 3 + i, h] = 1.0
            kones[0, BIAS_SLOTS * h + 3 + i] = 1.0
    return (jnp.asarray(pk, BF16), jnp.asarray(kones), jnp.asarray(pq, BF16), jnp.asarray(qones))


def kernel(x, norm1_g, w_in, sgu_ln_g, sgu_ln_b, w_spatial, b_spatial, b_forget, b_gate,
           w_proj_a, w_proj_b, w_out, norm_f_g):
    B, S, D = x.shape
    assert D == D_MODEL and S % TS == 0 and TQ == TK and TS == TK
    H = FOX_HEADS

    w_bf = w_in.astype(BF16)
    w_sgu = w_bf[:, :3 * D]
    w_q, w_k, w_v, w_zb = (w_bf[:, (3 + i) * D:(4 + i) * D] for i in range(4))
    w_kz = jnp.concatenate([w_k, w_zb], axis=1)
    w_qvt = jnp.concatenate([w_q, w_v], axis=1).T
    w_f = jnp.pad(w_bf[:, 7 * D:7 * D + H], ((0, 0), (0, LANES - H)))
    w_g = w_bf[:, 7 * D + H:]
    b_f = jnp.pad(b_forget, (0, LANES - H)).reshape(1, LANES)
    row = lambda t: t.reshape(1, -1)
    pk, kones, pq, qones = _bias_slot_constants()

    NS = S // TS
    tok = lambda: pl.BlockSpec((1, TS, D), lambda b, s: (b, s, 0))
    feat = lambda rows: pl.BlockSpec((1, 1, rows, TS), lambda b, s: (b, s, 0, 0))
    tok_out = jax.ShapeDtypeStruct((B, S, D), BF16)
    feat_out = lambda rows: jax.ShapeDtypeStruct((B, NS, rows, TS), BF16)
    qt, qbt, k, kb, vt, zb, gb, ma = pl.pallas_call(
        _proj_kernel,
        grid=(B, NS),
        in_specs=[
            tok(),
            _resident((1, D)), _resident((D, 3 * D)), _resident((D, 2 * D)), _resident((2 * D, D)),
            _resident((D, LANES)), _resident((D, 2 * D)), _resident((1, D)), _resident((1, D)),
            _resident((A_GROUPS, SGU_BLOCK, SGU_BLOCK)), _resident((SGU_BLOCK, A_GROUPS)),
            _resident((1, LANES)), _resident((1, 2 * D)), _resident((D, D)),
            _resident((3, LANES, LANES)), _resident((1, LANES)), _resident((3, LANES, LANES)),
            _resident((LANES, TS)),
        ],
        out_specs=[feat(D), feat(LANES), tok(),
                   pl.BlockSpec((1, TS, LANES), lambda b, s: (b, s, 0)),
                   feat(D), tok(), tok(), tok()],
        out_shape=[feat_out(D), feat_out(LANES), tok_out,
                   jax.ShapeDtypeStruct((B, S, LANES), BF16),
                   feat_out(D), tok_out, tok_out, tok_out],
        scratch_shapes=[pltpu.VMEM((TS, D), BF16), pltpu.VMEM((1, LANES), F32)],
        compiler_params=pltpu.CompilerParams(
            dimension_semantics=("arbitrary", "arbitrary"), vmem_limit_bytes=VMEM_LIMIT_BYTES),
        name="proj_sgu",
    )(x, row(norm1_g), w_sgu, w_kz, w_qvt, w_f, w_g, row(sgu_ln_g), row(sgu_ln_b), w_spatial,
      b_spatial.T, b_f, b_gate.reshape(1, 2 * D), w_proj_a.astype(BF16), pk, kones, pq, qones)

    qtok = lambda: pl.BlockSpec((1, TQ, D), lambda b, s: (b, s, 0))
    per_batch = lambda shape: pl.BlockSpec((1,) + shape, lambda b, s: (b,) + (0,) * len(shape),
                                           pipeline_mode=pl.Buffered(1))
    return pl.pallas_call(
        _attn_kernel,
        grid=(B, S // TQ),
        in_specs=[
            feat(D), feat(LANES),
            per_batch((S, D)), per_batch((S, LANES)), per_batch((NS, D, TK)),
            qtok(), qtok(), qtok(), qtok(),
            _resident((D, D)), _resident((D, D)), _resident((1, D)),
        ],
        out_specs=qtok(),
        out_shape=jax.ShapeDtypeStruct((B, S, D), x.dtype),
        scratch_shapes=[pltpu.VMEM((TQ, D), BF16),
                        pltpu.VMEM((H, FOX_HEAD_DIM + LANES, TQ), BF16),
                        pltpu.VMEM((H, 8, TQ), F32),
                        pltpu.VMEM((H, ACC_ROWS, TQ), F32)],
        compiler_params=pltpu.CompilerParams(
            dimension_semantics=("arbitrary", "arbitrary"), vmem_limit_bytes=VMEM_LIMIT_BYTES),
        name="fox_merge",
    )(qt, qbt, k, kb, vt, zb, gb, ma, x, w_proj_b.astype(BF16), w_out.astype(BF16),
      row(norm_f_g))
```

```python
import math

import numpy as np
import jax
import jax.numpy as jnp
from jax import lax
from jax.experimental import pallas as pl
from jax.experimental.pallas import tpu as pltpu

D_MODEL = 1024
CHUNK = 64
SGU_BLOCK = 128
A_GROUPS = 8
A_GROUP_DIM = D_MODEL // A_GROUPS
FOX_HEADS = 8
FOX_HEAD_DIM = D_MODEL // FOX_HEADS
EPS = 1e-6
NEG_INF = -1e30
LOG2E = math.log2(math.e)

LANES = 128
BF16_ROWS = 16
TS = 256
TQ = 256
TK = 256
VMEM_LIMIT_BYTES = 56 * 1024 * 1024
BIAS_SLOTS = LANES // FOX_HEADS
ACC_ROWS = FOX_HEAD_DIM + BF16_ROWS

F32 = jnp.float32
BF16 = jnp.bfloat16
_NT = (((1,), (1,)), ((), ()))


def _dot(a, b):
    return jnp.dot(a, b, preferred_element_type=F32)


def _dot_nt(a, b):
    return lax.dot_general(a, b, _NT, preferred_element_type=F32)


def _sigmoid(t):
    return 1.0 / (1.0 + jnp.exp(-t))


def _split3(t):
    hi = t.astype(BF16)
    r1 = t - hi.astype(F32)
    mid = r1.astype(BF16)
    lo = (r1 - mid.astype(F32)).astype(BF16)
    return hi, mid, lo


def _proj_kernel(x_ref, g1_ref, wsgu_ref, wkz_ref, wqvt_ref, wf_ref, wg_ref, lng_ref, lnb_ref,
                 ws_ref, bst_ref, bf_ref, bg_ref, wpa_ref, pk_ref, kones_ref, pq_ref, qones_ref,
                 qt_ref, qbt_ref, k_ref, kb_ref, vt_ref, zb_ref, gb_ref, ma_ref,
                 a_scr, carry_scr):
    @pl.when(pl.program_id(1) == 0)
    def _():
        carry_scr[...] = jnp.zeros_like(carry_scr)

    x = x_ref[0]
    ms = jnp.mean(x * x, axis=-1, keepdims=True)
    xn = (x * lax.rsqrt(ms + EPS) * g1_ref[...]).astype(BF16)

    uvz = _dot(xn, wsgu_ref[...])
    pos = lax.broadcasted_iota(jnp.int32, (SGU_BLOCK, SGU_BLOCK), 0) // CHUNK
    src = lax.broadcasted_iota(jnp.int32, (SGU_BLOCK, SGU_BLOCK), 1) // CHUNK
    causal = pos >= src
    for g in range(A_GROUPS):
        c0 = g * A_GROUP_DIM
        ws = jnp.where(causal, ws_ref[g], 0.0).astype(BF16)
        bias = bst_ref[:, g:g + 1]
        lng = lng_ref[:, c0:c0 + A_GROUP_DIM]
        lnb = lnb_ref[:, c0:c0 + A_GROUP_DIM]
        for n in range(TS // SGU_BLOCK):
            r0 = n * SGU_BLOCK
            u = uvz[r0:r0 + SGU_BLOCK, c0:c0 + A_GROUP_DIM]
            v = uvz[r0:r0 + SGU_BLOCK, D_MODEL + c0:D_MODEL + c0 + A_GROUP_DIM]
            z = uvz[r0:r0 + SGU_BLOCK, 2 * D_MODEL + c0:2 * D_MODEL + c0 + A_GROUP_DIM]
            mu = jnp.mean(v, axis=-1, keepdims=True)
            d = v - mu
            var = jnp.mean(d * d, axis=-1, keepdims=True)
            vn = ((d * lax.rsqrt(var + EPS)) * lng + lnb).astype(BF16)
            y = _dot(ws, vn) + bias
            a = u * y * (z * _sigmoid(z))
            a_scr[r0:r0 + SGU_BLOCK, c0:c0 + A_GROUP_DIM] = a.astype(BF16)

    gates = _dot(xn, wg_ref[...]) + bg_ref[...]
    ma = _sigmoid(gates[:, :D_MODEL]) * _dot(a_scr[...], wpa_ref[...])
    ma_ref[0] = ma.astype(BF16)
    gb_ref[0] = _sigmoid(gates[:, D_MODEL:]).astype(BF16)

    kz = _dot(xn, wkz_ref[...])
    k_ref[0] = kz[:, :D_MODEL].astype(BF16)
    zb = kz[:, D_MODEL:]
    zb_ref[0] = (zb * _sigmoid(zb)).astype(BF16)
    qvt = _dot_nt(wqvt_ref[...], xn)
    qt_ref[0, 0] = (qvt[:D_MODEL] * (LOG2E / math.sqrt(FOX_HEAD_DIM))).astype(BF16)
    vt_ref[0, 0] = qvt[D_MODEL:].astype(BF16)

    f = _dot(xn, wf_ref[...]) + bf_ref[...]
    logf = jnp.minimum(f, 0.0) - jnp.log1p(jnp.exp(-jnp.abs(f)))
    tri = (lax.broadcasted_iota(jnp.int32, (TS, TS), 0)
           >= lax.broadcasted_iota(jnp.int32, (TS, TS), 1)).astype(BF16)
    c = carry_scr[...] + sum(_dot(tri, t) for t in _split3(logf))
    carry_scr[...] = c[TS - 1:TS, :]
    parts = _split3(c * LOG2E)
    kb = kones_ref[...] + sum(_dot(t, pk_ref[i]) for i, t in enumerate(parts))
    kb_ref[0] = kb.astype(BF16)
    qb = qones_ref[...] + sum(_dot_nt(pq_ref[i], t) for i, t in enumerate(parts))
    qbt_ref[0, 0] = qb.astype(BF16)


def _attn_kernel(qt_ref, qbt_ref, k_ref, kb_ref, vt_ref, zb_ref, gb_ref, ma_ref, x_ref,
                 wpb_ref, wout_ref, gf_ref, out_ref, o_scr, qa_scr, m_scr, acc_scr, sa_scr, sb_scr):
    qi = pl.program_id(1)
    kpos = lax.broadcasted_iota(jnp.int32, (TK, TQ), 0)
    qpos = lax.broadcasted_iota(jnp.int32, (TK, TQ), 1)
    diag_mask = kpos <= qpos
    slot_head = lax.broadcasted_iota(jnp.int32, (LANES, TQ), 0) // BIAS_SLOTS
    qbt = qbt_ref[0, 0]
    for h in range(FOX_HEADS):
        c0 = h * FOX_HEAD_DIM
        qa_scr[h, :FOX_HEAD_DIM, :] = qt_ref[0, 0, c0:c0 + FOX_HEAD_DIM, :]
        qa_scr[h, FOX_HEAD_DIM:, :] = jnp.where(slot_head == h, qbt, jnp.zeros_like(qbt))
    m_scr[...] = jnp.full(m_scr.shape, NEG_INF, F32)
    acc_scr[...] = jnp.zeros_like(acc_scr)
    ones_rows = jnp.ones((BF16_ROWS, TK), BF16)

    def scores(j, h):
        c0 = h * FOX_HEAD_DIM
        k0 = pl.multiple_of(j * TK, TK)
        ka = jnp.concatenate([k_ref[0, pl.ds(k0, TK), c0:c0 + FOX_HEAD_DIM],
                              kb_ref[0, pl.ds(k0, TK), :]], axis=1)
        return _dot(ka, qa_scr[h])

    def update(j, h, st):
        c0 = h * FOX_HEAD_DIM
        m_old = m_scr[h, 0:1, :]
        m_new = jnp.maximum(m_old, jnp.max(st, axis=0, keepdims=True))
        alpha = jnp.exp2(m_old - m_new)
        pt = jnp.exp2(st - m_new).astype(BF16)
        m_scr[h, 0:1, :] = m_new
        va = jnp.concatenate([vt_ref[0, j, c0:c0 + FOX_HEAD_DIM, :], ones_rows], axis=0)
        acc_scr[h] = alpha * acc_scr[h] + _dot(va, pt)

    def stage(j_next, s_next, j, s_cur, masked=False):
        if j_next is not None:
            for h in range(FOX_HEADS):
                s_next[h] = scores(j_next, h)
        for h in range(FOX_HEADS):
            st = s_cur[h]
            update(j, h, jnp.where(diag_mask, st, NEG_INF) if masked else st)

    for h in range(FOX_HEADS):
        sa_scr[h] = scores(0, h)

    @pl.loop(0, qi // 2)
    def _(t):
        j = 2 * t
        stage(j + 1, sb_scr, j, sa_scr)
        stage(j + 2, sa_scr, j + 1, sb_scr)

    @pl.when(qi % 2 == 0)
    def _():
        stage(None, None, qi, sa_scr, masked=True)

    @pl.when(qi % 2 == 1)
    def _():
        stage(qi, sb_scr, qi - 1, sa_scr)
        stage(None, None, qi, sb_scr, masked=True)

    for h in range(FOX_HEADS):
        c0 = h * FOX_HEAD_DIM
        acc = acc_scr[h]
        ot = acc[:FOX_HEAD_DIM] / acc[FOX_HEAD_DIM:FOX_HEAD_DIM + 1]
        o = ot.T * zb_ref[0, :, c0:c0 + FOX_HEAD_DIM].astype(F32)
        o_scr[:, c0:c0 + FOX_HEAD_DIM] = o.astype(BF16)

    pb = _dot(o_scr[...], wpb_ref[...])
    merged = gb_ref[0].astype(F32) * pb + ma_ref[0].astype(F32)
    hres = x_ref[0] + _dot(merged.astype(BF16), wout_ref[...])
    ms = jnp.mean(hres * hres, axis=-1, keepdims=True)
    out_ref[0] = hres * lax.rsqrt(ms + EPS) * gf_ref[...]


def _resident(shape):
    return pl.BlockSpec(shape, lambda b, s: (0,) * len(shape), pipeline_mode=pl.Buffered(1))


def _bias_slot_constants():
    pk = np.zeros((3, LANES, LANES), np.float32)
    pq = np.zeros((3, LANES, LANES), np.float32)
    kones = np.zeros((1, LANES), np.float32)
    qones = np.zeros((LANES, TS), np.float32)
    for h in range(FOX_HEADS):
        for i in range(3):
            pk[i, h, BIAS_SLOTS * h + i] = -1.0
            qones[BIAS_SLOTS * h + i, :] = 1.0
            pq[i, BIAS_SLOTS * h + 3 + i, h] = 1.0
            kones[0, BIAS_SLOTS * h + 3 + i] = 1.0
    return (jnp.asarray(pk, BF16), jnp.asarray(kones), jnp.asarray(pq, BF16), jnp.asarray(qones))


def kernel(x, norm1_g, w_in, sgu_ln_g, sgu_ln_b, w_spatial, b_spatial, b_forget, b_gate,
           w_proj_a, w_proj_b, w_out, norm_f_g):
    B, S, D = x.shape
    assert D == D_MODEL and S % TS == 0 and TQ == TK and TS == TK
    H = FOX_HEADS

    w_bf = w_in.astype(BF16)
    w_sgu = w_bf[:, :3 * D]
    w_q, w_k, w_v, w_zb = (w_bf[:, (3 + i) * D:(4 + i) * D] for i in range(4))
    w_kz = jnp.concatenate([w_k, w_zb], axis=1)
    w_qvt = jnp.concatenate([w_q, w_v], axis=1).T
    w_f = jnp.pad(w_bf[:, 7 * D:7 * D + H], ((0, 0), (0, LANES - H)))
    w_g = w_bf[:, 7 * D + H:]
    b_f = jnp.pad(b_forget, (0, LANES - H)).reshape(1, LANES)
    row = lambda t: t.reshape(1, -1)
    pk, kones, pq, qones = _bias_slot_constants()

    NS = S // TS
    tok = lambda: pl.BlockSpec((1, TS, D), lambda b, s: (b, s, 0))
    feat = lambda rows: pl.BlockSpec((1, 1, rows, TS), lambda b, s: (b, s, 0, 0))
    tok_out = jax.ShapeDtypeStruct((B, S, D), BF16)
    feat_out = lambda rows: jax.ShapeDtypeStruct((B, NS, rows, TS), BF16)
    qt, qbt, k, kb, vt, zb, gb, ma = pl.pallas_call(
        _proj_kernel,
        grid=(B, NS),
        in_specs=[
            tok(),
            _resident((1, D)), _resident((D, 3 * D)), _resident((D, 2 * D)), _resident((2 * D, D)),
            _resident((D, LANES)), _resident((D, 2 * D)), _resident((1, D)), _resident((1, D)),
            _resident((A_GROUPS, SGU_BLOCK, SGU_BLOCK)), _resident((SGU_BLOCK, A_GROUPS)),
            _resident((1, LANES)), _resident((1, 2 * D)), _resident((D, D)),
            _resident((3, LANES, LANES)), _resident((1, LANES)), _resident((3, LANES, LANES)),
            _resident((LANES, TS)),
        ],
        out_specs=[feat(D), feat(LANES), tok(),
                   pl.BlockSpec((1, TS, LANES), lambda b, s: (b, s, 0)),
                   feat(D), tok(), tok(), tok()],
        out_shape=[feat_out(D), feat_out(LANES), tok_out,
                   jax.ShapeDtypeStruct((B, S, LANES), BF16),
                   feat_out(D), tok_out, tok_out, tok_out],
        scratch_shapes=[pltpu.VMEM((TS, D), BF16), pltpu.VMEM((1, LANES), F32)],
        compiler_params=pltpu.CompilerParams(
            dimension_semantics=("arbitrary", "arbitrary"), vmem_limit_bytes=VMEM_LIMIT_BYTES),
        name="proj_sgu",
    )(x, row(norm1_g), w_sgu, w_kz, w_qvt, w_f, w_g, row(sgu_ln_g), row(sgu_ln_b), w_spatial,
      b_spatial.T, b_f, b_gate.reshape(1, 2 * D), w_proj_a.astype(BF16), pk, kones, pq, qones)

    qtok = lambda: pl.BlockSpec((1, TQ, D), lambda b, s: (b, s, 0))
    per_batch = lambda shape: pl.BlockSpec((1,) + shape, lambda b, s: (b,) + (0,) * len(shape),
                                           pipeline_mode=pl.Buffered(1))
    return pl.pallas_call(
        _attn_kernel,
        grid=(B, S // TQ),
        in_specs=[
            feat(D), feat(LANES),
            per_batch((S, D)), per_batch((S, LANES)), per_batch((NS, D, TK)),
            qtok(), qtok(), qtok(), qtok(),
            _resident((D, D)), _resident((D, D)), _resident((1, D)),
        ],
        out_specs=qtok(),
        out_shape=jax.ShapeDtypeStruct((B, S, D), x.dtype),
        scratch_shapes=[pltpu.VMEM((TQ, D), BF16),
                        pltpu.VMEM((H, FOX_HEAD_DIM + LANES, TQ), BF16),
                        pltpu.VMEM((H, 8, TQ), F32),
                        pltpu.VMEM((H, ACC_ROWS, TQ), F32),
                        pltpu.VMEM((H, TK, TQ), F32), pltpu.VMEM((H, TK, TQ), F32)],
        compiler_params=pltpu.CompilerParams(
            dimension_semantics=("arbitrary", "arbitrary"), vmem_limit_bytes=VMEM_LIMIT_BYTES),
        name="fox_merge",
    )(qt, qbt, k, kb, vt, zb, gb, ma, x, w_proj_b.astype(BF16), w_out.astype(BF16),
      row(norm_f_g))
```

```python
import math

import numpy as np
import jax
import jax.numpy as jnp
from jax import lax
from jax.experimental import pallas as pl
from jax.experimental.pallas import tpu as pltpu

D_MODEL = 1024
CHUNK = 64
SGU_BLOCK = 128
A_GROUPS = 8
A_GROUP_DIM = D_MODEL // A_GROUPS
FOX_HEADS = 8
FOX_HEAD_DIM = D_MODEL // FOX_HEADS
EPS = 1e-6
NEG_INF = -1e30
LOG2E = math.log2(math.e)

LANES = 128
BF16_ROWS = 16
TS = 256
TQ = 256
TK = 256
VMEM_LIMIT_BYTES = 60 * 1024 * 1024
BIAS_SLOTS = LANES // FOX_HEADS
ACC_ROWS = FOX_HEAD_DIM + BF16_ROWS

F32 = jnp.float32
BF16 = jnp.bfloat16
_NT = (((1,), (1,)), ((), ()))


def _dot(a, b):
    return jnp.dot(a, b, preferred_element_type=F32)


def _dot_nt(a, b):
    return lax.dot_general(a, b, _NT, preferred_element_type=F32)


def _sigmoid(t):
    return 1.0 / (1.0 + jnp.exp(-t))


def _split3(t):
    hi = t.astype(BF16)
    r1 = t - hi.astype(F32)
    mid = r1.astype(BF16)
    lo = (r1 - mid.astype(F32)).astype(BF16)
    return hi, mid, lo


def _proj_kernel(x_ref, g1_ref, wsgu_ref, wkz_ref, wqvt_ref, wf_ref, wg_ref, lng_ref, lnb_ref,
                 ws_ref, bst_ref, bf_ref, bg_ref, wpa_ref, pk_ref, kones_ref, pq_ref, qones_ref,
                 qt_ref, qbt_ref, k_ref, kb_ref, vt_ref, zb_ref, gb_ref, ma_ref,
                 a_scr, carry_scr):
    @pl.when(pl.program_id(1) == 0)
    def _():
        carry_scr[...] = jnp.zeros_like(carry_scr)

    x = x_ref[0]
    ms = jnp.mean(x * x, axis=-1, keepdims=True)
    xn = (x * lax.rsqrt(ms + EPS) * g1_ref[...]).astype(BF16)

    uvz = _dot(xn, wsgu_ref[...])
    pos = lax.broadcasted_iota(jnp.int32, (SGU_BLOCK, SGU_BLOCK), 0) // CHUNK
    src = lax.broadcasted_iota(jnp.int32, (SGU_BLOCK, SGU_BLOCK), 1) // CHUNK
    causal = pos >= src
    for g in range(A_GROUPS):
        c0 = g * A_GROUP_DIM
        ws = jnp.where(causal, ws_ref[g], 0.0).astype(BF16)
        bias = bst_ref[:, g:g + 1]
        lng = lng_ref[:, c0:c0 + A_GROUP_DIM]
        lnb = lnb_ref[:, c0:c0 + A_GROUP_DIM]
        for n in range(TS // SGU_BLOCK):
            r0 = n * SGU_BLOCK
            u = uvz[r0:r0 + SGU_BLOCK, c0:c0 + A_GROUP_DIM]
            v = uvz[r0:r0 + SGU_BLOCK, D_MODEL + c0:D_MODEL + c0 + A_GROUP_DIM]
            z = uvz[r0:r0 + SGU_BLOCK, 2 * D_MODEL + c0:2 * D_MODEL + c0 + A_GROUP_DIM]
            mu = jnp.mean(v, axis=-1, keepdims=True)
            d = v - mu
            var = jnp.mean(d * d, axis=-1, keepdims=True)
            vn = ((d * lax.rsqrt(var + EPS)) * lng + lnb).astype(BF16)
            y = _dot(ws, vn) + bias
            a = u * y * (z * _sigmoid(z))
            a_scr[r0:r0 + SGU_BLOCK, c0:c0 + A_GROUP_DIM] = a.astype(BF16)

    gates = _dot(xn, wg_ref[...]) + bg_ref[...]
    ma = _sigmoid(gates[:, :D_MODEL]) * _dot(a_scr[...], wpa_ref[...])
    ma_ref[0] = ma.astype(BF16)
    gb_ref[0] = _sigmoid(gates[:, D_MODEL:]).astype(BF16)

    kz = _dot(xn, wkz_ref[...])
    k_ref[0] = kz[:, :D_MODEL].astype(BF16)
    zb = kz[:, D_MODEL:]
    zb_ref[0] = (zb * _sigmoid(zb)).astype(BF16)
    qvt = _dot_nt(wqvt_ref[...], xn)
    qt_ref[0, 0] = (qvt[:D_MODEL] * (LOG2E / math.sqrt(FOX_HEAD_DIM))).astype(BF16)
    vt_ref[0, 0] = qvt[D_MODEL:].astype(BF16)

    f = _dot(xn, wf_ref[...]) + bf_ref[...]
    logf = jnp.minimum(f, 0.0) - jnp.log1p(jnp.exp(-jnp.abs(f)))
    tri = (lax.broadcasted_iota(jnp.int32, (TS, TS), 0)
           >= lax.broadcasted_iota(jnp.int32, (TS, TS), 1)).astype(BF16)
    c = carry_scr[...] + sum(_dot(tri, t) for t in _split3(logf))
    carry_scr[...] = c[TS - 1:TS, :]
    parts = _split3(c * LOG2E)
    kb = kones_ref[...] + sum(_dot(t, pk_ref[i]) for i, t in enumerate(parts))
    kb_ref[0] = kb.astype(BF16)
    qb = qones_ref[...] + sum(_dot_nt(pq_ref[i], t) for i, t in enumerate(parts))
    qbt_ref[0, 0] = qb.astype(BF16)


def _attn_kernel(qt_ref, qbt_ref, k_ref, kb_ref, vt_ref, zb_ref, gb_ref, ma_ref, x_ref,
                 wpb_ref, wout_ref, gf_ref, out_ref, o_scr, qa_scr, m_scr, acc_scr, sa_scr, sb_scr):
    qi = pl.program_id(1)
    kpos = lax.broadcasted_iota(jnp.int32, (TK, TQ), 0)
    qpos = lax.broadcasted_iota(jnp.int32, (TK, TQ), 1)
    diag_mask = kpos <= qpos
    slot_head = lax.broadcasted_iota(jnp.int32, (LANES, TQ), 0) // BIAS_SLOTS
    qbt = qbt_ref[0, 0]
    for h in range(FOX_HEADS):
        c0 = h * FOX_HEAD_DIM
        qa_scr[h, :FOX_HEAD_DIM, :] = qt_ref[0, 0, c0:c0 + FOX_HEAD_DIM, :]
        qa_scr[h, FOX_HEAD_DIM:, :] = jnp.where(slot_head == h, qbt, jnp.zeros_like(qbt))
    m_scr[...] = jnp.full(m_scr.shape, NEG_INF, F32)
    acc_scr[...] = jnp.zeros_like(acc_scr)
    ones_rows = jnp.ones((BF16_ROWS, TK), BF16)

    def scores(j, h):
        c0 = h * FOX_HEAD_DIM
        k0 = pl.multiple_of(j * TK, TK)
        ka = jnp.concatenate([k_ref[0, pl.ds(k0, TK), c0:c0 + FOX_HEAD_DIM],
                              kb_ref[0, pl.ds(k0, TK), :]], axis=1)
        return _dot(ka, qa_scr[h])

    def update(j, h, st):
        c0 = h * FOX_HEAD_DIM
        m_old = m_scr[h, 0:1, :]
        m_new = jnp.maximum(m_old, jnp.max(st, axis=0, keepdims=True))
        alpha = jnp.exp2(m_old - m_new)
        pt = jnp.exp2(st - m_new).astype(BF16)
        m_scr[h, 0:1, :] = m_new
        va = jnp.concatenate([vt_ref[0, j, c0:c0 + FOX_HEAD_DIM, :], ones_rows], axis=0)
        acc_scr[h] = alpha * acc_scr[h] + _dot(va, pt)

    def stage(j_next, s_next, j, s_cur, masked=False):
        if j_next is not None:
            for h in range(FOX_HEADS):
                s_next[h] = scores(j_next, h)
        for h in range(FOX_HEADS):
            st = s_cur[h]
            update(j, h, jnp.where(diag_mask, st, NEG_INF) if masked else st)

    for h in range(FOX_HEADS):
        sa_scr[h] = scores(0, h)

    @pl.loop(0, qi // 2)
    def _(t):
        j = 2 * t
        stage(j + 1, sb_scr, j, sa_scr)
        stage(j + 2, sa_scr, j + 1, sb_scr)

    @pl.when(qi % 2 == 0)
    def _():
        stage(None, None, qi, sa_scr, masked=True)

    @pl.when(qi % 2 == 1)
    def _():
        stage(qi, sb_scr, qi - 1, sa_scr)
        stage(None, None, qi, sb_scr, masked=True)

    for h in range(FOX_HEADS):
        c0 = h * FOX_HEAD_DIM
        acc = acc_scr[h]
        ot = acc[:FOX_HEAD_DIM] / acc[FOX_HEAD_DIM:FOX_HEAD_DIM + 1]
        o = ot.T * zb_ref[0, :, c0:c0 + FOX_HEAD_DIM].astype(F32)
        o_scr[:, c0:c0 + FOX_HEAD_DIM] = o.astype(BF16)

    pb = _dot(o_scr[...], wpb_ref[...])
    merged = gb_ref[0].astype(F32) * pb + ma_ref[0].astype(F32)
    hres = x_ref[0] + _dot(merged.astype(BF16), wout_ref[...])
    ms = jnp.mean(hres * hres, axis=-1, keepdims=True)
    out_ref[0] = hres * lax.rsqrt(ms + EPS) * gf_ref[...]


def _resident(shape):
    return pl.BlockSpec(shape, lambda b, s: (0,) * len(shape), pipeline_mode=pl.Buffered(1))


def _bias_slot_constants():
    pk = np.zeros((3, LANES, LANES), np.float32)
    pq = np.zeros((3, LANES, LANES), np.float32)
    kones = np.zeros((1, LANES), np.float32)
    qones = np.zeros((LANES, TS), np.float32)
    for h in range(FOX_HEADS):
        for i in range(3):
            pk[i, h, BIAS_SLOTS * h + i] = -1.0
            qones[BIAS_SLOTS * h + i, :] = 1.0
            pq[i, BIAS_SLOTS * h + 3 + i, h] = 1.0
            kones[0, BIAS_SLOTS * h + 3 + i] = 1.0
    return (jnp.asarray(pk, BF16), jnp.asarray(kones), jnp.asarray(pq, BF16), jnp.asarray(qones))


def kernel(x, norm1_g, w_in, sgu_ln_g, sgu_ln_b, w_spatial, b_spatial, b_forget, b_gate,
           w_proj_a, w_proj_b, w_out, norm_f_g):
    B, S, D = x.shape
    assert D == D_MODEL and S % TS == 0 and TQ == TK and TS == TK
    H = FOX_HEADS

    w_sgu = w_in[:, :3 * D].astype(BF16)
    w_q, w_k, w_v, w_zb = (w_in[:, (3 + i) * D:(4 + i) * D] for i in range(4))
    w_kz = jnp.concatenate([w_k, w_zb], axis=1).astype(BF16)
    w_qvt = jnp.concatenate([w_q, w_v], axis=1).T.astype(BF16)
    w_f = jnp.pad(w_in[:, 7 * D:7 * D + H].astype(BF16), ((0, 0), (0, LANES - H)))
    w_g = w_in[:, 7 * D + H:].astype(BF16)
    b_f = jnp.pad(b_forget, (0, LANES - H)).reshape(1, LANES)
    row = lambda t: t.reshape(1, -1)
    pk, kones, pq, qones = _bias_slot_constants()

    NS = S // TS
    tok = lambda: pl.BlockSpec((1, TS, D), lambda b, s: (b, s, 0))
    feat = lambda rows: pl.BlockSpec((1, 1, rows, TS), lambda b, s: (b, s, 0, 0))
    tok_out = jax.ShapeDtypeStruct((B, S, D), BF16)
    feat_out = lambda rows: jax.ShapeDtypeStruct((B, NS, rows, TS), BF16)
    qt, qbt, k, kb, vt, zb, gb, ma = pl.pallas_call(
        _proj_kernel,
        grid=(B, NS),
        in_specs=[
            tok(),
            _resident((1, D)), _resident((D, 3 * D)), _resident((D, 2 * D)), _resident((2 * D, D)),
            _resident((D, LANES)), _resident((D, 2 * D)), _resident((1, D)), _resident((1, D)),
            _resident((A_GROUPS, SGU_BLOCK, SGU_BLOCK)), _resident((SGU_BLOCK, A_GROUPS)),
            _resident((1, LANES)), _resident((1, 2 * D)), _resident((D, D)),
            _resident((3, LANES, LANES)), _resident((1, LANES)), _resident((3, LANES, LANES)),
            _resident((LANES, TS)),
        ],
        out_specs=[feat(D), feat(LANES), tok(),
                   pl.BlockSpec((1, TS, LANES), lambda b, s: (b, s, 0)),
                   feat(D), tok(), tok(), tok()],
        out_shape=[feat_out(D), feat_out(LANES), tok_out,
                   jax.ShapeDtypeStruct((B, S, LANES), BF16),
                   feat_out(D), tok_out, tok_out, tok_out],
        scratch_shapes=[pltpu.VMEM((TS, D), BF16), pltpu.VMEM((1, LANES), F32)],
        compiler_params=pltpu.CompilerParams(
            dimension_semantics=("arbitrary", "arbitrary"), vmem_limit_bytes=VMEM_LIMIT_BYTES),
        name="proj_sgu",
    )(x, row(norm1_g), w_sgu, w_kz, w_qvt, w_f, w_g, row(sgu_ln_g), row(sgu_ln_b), w_spatial,
      b_spatial.T, b_f, b_gate.reshape(1, 2 * D), w_proj_a.astype(BF16), pk, kones, pq, qones)

    qtok = lambda: pl.BlockSpec((1, TQ, D), lambda b, s: (b, s, 0))
    per_batch = lambda shape: pl.BlockSpec((1,) + shape, lambda b, s: (b,) + (0,) * len(shape))
    return pl.pallas_call(
        _attn_kernel,
        grid=(B, S // TQ),
        in_specs=[
            feat(D), feat(LANES),
            per_batch((S, D)), per_batch((S, LANES)), per_batch((NS, D, TK)),
            qtok(), qtok(), qtok(), qtok(),
            _resident((D, D)), _resident((D, D)), _resident((1, D)),
        ],
        out_specs=qtok(),
        out_shape=jax.ShapeDtypeStruct((B, S, D), x.dtype),
        scratch_shapes=[pltpu.VMEM((TQ, D), BF16),
                        pltpu.VMEM((H, FOX_HEAD_DIM + LANES, TQ), BF16),
                        pltpu.VMEM((H, 8, TQ), F32),
                        pltpu.VMEM((H, ACC_ROWS, TQ), F32),
                        pltpu.VMEM((H, TK, TQ), F32), pltpu.VMEM((H, TK, TQ), F32)],
        compiler_params=pltpu.CompilerParams(
            dimension_semantics=("arbitrary", "arbitrary"), vmem_limit_bytes=VMEM_LIMIT_BYTES),
        name="fox_merge",
    )(qt, qbt, k, kb, vt, zb, gb, ma, x, w_proj_b.astype(BF16), w_out.astype(BF16),
      row(norm_f_g))
```

```python
import math

import numpy as np
import jax
import jax.numpy as jnp
from jax import lax
from jax.experimental import pallas as pl
from jax.experimental.pallas import tpu as pltpu

D_MODEL = 1024
CHUNK = 64
SGU_BLOCK = 128
A_GROUPS = 8
A_GROUP_DIM = D_MODEL // A_GROUPS
FOX_HEADS = 8
FOX_HEAD_DIM = D_MODEL // FOX_HEADS
EPS = 1e-6
NEG_INF = -1e30
LOG2E = math.log2(math.e)

LANES = 128
BF16_ROWS = 16
TS = 256
TQ = 256
TK = 256
VMEM_LIMIT_BYTES = 60 * 1024 * 1024
BIAS_SLOTS = LANES // FOX_HEADS
ACC_ROWS = FOX_HEAD_DIM + BF16_ROWS

F32 = jnp.float32
BF16 = jnp.bfloat16
_NT = (((1,), (1,)), ((), ()))
_TN = (((0,), (0,)), ((), ()))


def _dot(a, b):
    return jnp.dot(a, b, preferred_element_type=F32)


def _dot_nt(a, b):
    return lax.dot_general(a, b, _NT, preferred_element_type=F32)


def _sigmoid(t):
    return 1.0 / (1.0 + jnp.exp(-t))


def _split3_rows(t):
    hi = t.astype(BF16).astype(F32)
    r1 = t - hi
    mid = r1.astype(BF16).astype(F32)
    lo = (r1 - mid).astype(BF16).astype(F32)
    pad = jnp.zeros((LANES - 3 * t.shape[0], t.shape[1]), F32)
    return jnp.concatenate([hi, mid, lo, pad], axis=0).astype(BF16)


def _proj_kernel(x_ref, g1_ref, wsgu_ref, wkz_ref, wqvft_ref, wg_ref, lng_ref, lnb_ref,
                 ws_ref, bst_ref, bft_ref, bg_ref, wpa_ref, pk_ref, kones_ref, pq_ref, qones_ref,
                 qt_ref, qbt_ref, k_ref, kb_ref, vt_ref, zb_ref, gb_ref, ma_ref,
                 a_scr, carry_scr):
    @pl.when(pl.program_id(1) == 0)
    def _():
        carry_scr[...] = jnp.zeros_like(carry_scr)

    x = x_ref[0]
    ms = jnp.mean(x * x, axis=-1, keepdims=True)
    xn = (x * lax.rsqrt(ms + EPS) * g1_ref[...]).astype(BF16)

    uvz = _dot(xn, wsgu_ref[...])
    pos = lax.broadcasted_iota(jnp.int32, (SGU_BLOCK, SGU_BLOCK), 0) // CHUNK
    src = lax.broadcasted_iota(jnp.int32, (SGU_BLOCK, SGU_BLOCK), 1) // CHUNK
    causal = pos >= src
    for g in range(A_GROUPS):
        c0 = g * A_GROUP_DIM
        ws = jnp.where(causal, ws_ref[g], 0.0).astype(BF16)
        bias = bst_ref[:, g:g + 1]
        lng = lng_ref[:, c0:c0 + A_GROUP_DIM]
        lnb = lnb_ref[:, c0:c0 + A_GROUP_DIM]
        vns = []
        for n in range(TS // SGU_BLOCK):
            r0 = n * SGU_BLOCK
            v = uvz[r0:r0 + SGU_BLOCK, D_MODEL + c0:D_MODEL + c0 + A_GROUP_DIM]
            mu = jnp.mean(v, axis=-1, keepdims=True)
            d = v - mu
            var = jnp.mean(d * d, axis=-1, keepdims=True)
            vns.append(((d * lax.rsqrt(var + EPS)) * lng + lnb).astype(BF16))
        y_all = _dot(ws, jnp.concatenate(vns, axis=1))
        for n in range(TS // SGU_BLOCK):
            r0 = n * SGU_BLOCK
            u = uvz[r0:r0 + SGU_BLOCK, c0:c0 + A_GROUP_DIM]
            z = uvz[r0:r0 + SGU_BLOCK, 2 * D_MODEL + c0:2 * D_MODEL + c0 + A_GROUP_DIM]
            y = y_all[:, n * A_GROUP_DIM:(n + 1) * A_GROUP_DIM] + bias
            a = u * y * (z * _sigmoid(z))
            a_scr[r0:r0 + SGU_BLOCK, c0:c0 + A_GROUP_DIM] = a.astype(BF16)

    gates = _dot(xn, wg_ref[...]) + bg_ref[...]
    ma = _sigmoid(gates[:, :D_MODEL]) * _dot(a_scr[...], wpa_ref[...])
    ma_ref[0] = ma.astype(BF16)
    gb_ref[0] = _sigmoid(gates[:, D_MODEL:]).astype(BF16)

    kz = _dot(xn, wkz_ref[...])
    k_ref[0] = kz[:, :D_MODEL].astype(BF16)
    zb = kz[:, D_MODEL:]
    zb_ref[0] = (zb * _sigmoid(zb)).astype(BF16)
    qvft = _dot_nt(wqvft_ref[...], xn)
    qt_ref[0, 0] = (qvft[:D_MODEL] * (LOG2E / math.sqrt(FOX_HEAD_DIM))).astype(BF16)
    vt_ref[0, 0] = qvft[D_MODEL:2 * D_MODEL].astype(BF16)

    f = qvft[2 * D_MODEL:2 * D_MODEL + FOX_HEADS] + bft_ref[...]
    logf = jnp.minimum(f, 0.0) - jnp.log1p(jnp.exp(-jnp.abs(f)))
    triu = (lax.broadcasted_iota(jnp.int32, (TS, TS), 0)
            <= lax.broadcasted_iota(jnp.int32, (TS, TS), 1)).astype(BF16)
    r = _dot(_split3_rows(logf), triu)
    c = carry_scr[...] + (r[0:FOX_HEADS] + r[FOX_HEADS:2 * FOX_HEADS]
                          + r[2 * FOX_HEADS:3 * FOX_HEADS])
    carry_scr[...] = jnp.broadcast_to(c[:, TS - 1:TS], c.shape)
    parts = _split3_rows(c * LOG2E)
    qb = qones_ref[...] + _dot(pq_ref[...], parts)
    qbt_ref[0, 0] = qb.astype(BF16)
    kb = kones_ref[...] + lax.dot_general(parts, pk_ref[...], _TN, preferred_element_type=F32)
    kb_ref[0] = kb.astype(BF16)


def _attn_kernel(qt_ref, qbt_ref, k_ref, kb_ref, vt_ref, zb_ref, gb_ref, ma_ref, x_ref,
                 wpb_ref, wout_ref, gf_ref, out_ref, o_scr, qa_scr, m_scr, acc_scr, sa_scr, sb_scr):
    qi = pl.program_id(1)
    kpos = lax.broadcasted_iota(jnp.int32, (TK, TQ), 0)
    qpos = lax.broadcasted_iota(jnp.int32, (TK, TQ), 1)
    diag_mask = kpos <= qpos
    slot_head = lax.broadcasted_iota(jnp.int32, (LANES, TQ), 0) // BIAS_SLOTS
    qbt = qbt_ref[0, 0]
    for h in range(FOX_HEADS):
        c0 = h * FOX_HEAD_DIM
        qa_scr[h, :FOX_HEAD_DIM, :] = qt_ref[0, 0, c0:c0 + FOX_HEAD_DIM, :]
        qa_scr[h, FOX_HEAD_DIM:, :] = jnp.where(slot_head == h, qbt, jnp.zeros_like(qbt))
    m_scr[...] = jnp.full(m_scr.shape, NEG_INF, F32)
    acc_scr[...] = jnp.zeros_like(acc_scr)
    ones_rows = jnp.ones((BF16_ROWS, TK), BF16)

    def scores(j, h):
        c0 = h * FOX_HEAD_DIM
        k0 = pl.multiple_of(j * TK, TK)
        ka = jnp.concatenate([k_ref[0, pl.ds(k0, TK), c0:c0 + FOX_HEAD_DIM],
                              kb_ref[0, pl.ds(k0, TK), :]], axis=1)
        return _dot(ka, qa_scr[h])

    def update(j, h, st):
        c0 = h * FOX_HEAD_DIM
        m_old = m_scr[h, 0:1, :]
        m_new = jnp.maximum(m_old, jnp.max(st, axis=0, keepdims=True))
        alpha = jnp.exp2(m_old - m_new)
        pt = jnp.exp2(st - m_new).astype(BF16)
        m_scr[h, 0:1, :] = m_new
        va = jnp.concatenate([vt_ref[0, j, c0:c0 + FOX_HEAD_DIM, :], ones_rows], axis=0)
        acc_scr[h] = alpha * acc_scr[h] + _dot(va, pt)

    def stage(j_next, s_next, j, s_cur, masked=False):
        if j_next is not None:
            for h in range(FOX_HEADS):
                s_next[h] = scores(j_next, h)
        for h in range(FOX_HEADS):
            st = s_cur[h]
            update(j, h, jnp.where(diag_mask, st, NEG_INF) if masked else st)

    for h in range(FOX_HEADS):
        sa_scr[h] = scores(0, h)

    @pl.loop(0, qi // 2)
    def _(t):
        j = 2 * t
        stage(j + 1, sb_scr, j, sa_scr)
        stage(j + 2, sa_scr, j + 1, sb_scr)

    @pl.when(qi % 2 == 0)
    def _():
        stage(None, None, qi, sa_scr, masked=True)

    @pl.when(qi % 2 == 1)
    def _():
        stage(qi, sb_scr, qi - 1, sa_scr)
        stage(None, None, qi, sb_scr, masked=True)

    for h in range(FOX_HEADS):
        c0 = h * FOX_HEAD_DIM
        acc = acc_scr[h]
        ot = acc[:FOX_HEAD_DIM] / acc[FOX_HEAD_DIM:FOX_HEAD_DIM + 1]
        o = ot.T * zb_ref[0, :, c0:c0 + FOX_HEAD_DIM].astype(F32)
        o_scr[:, c0:c0 + FOX_HEAD_DIM] = o.astype(BF16)

    pb = _dot(o_scr[...], wpb_ref[...])
    merged = gb_ref[0].astype(F32) * pb + ma_ref[0].astype(F32)
    hres = x_ref[0] + _dot(merged.astype(BF16), wout_ref[...])
    ms = jnp.mean(hres * hres, axis=-1, keepdims=True)
    out_ref[0] = hres * lax.rsqrt(ms + EPS) * gf_ref[...]


def _resident(shape):
    return pl.BlockSpec(shape, lambda b, s: (0,) * len(shape), pipeline_mode=pl.Buffered(1))


def _bias_slot_constants():
    pk = np.zeros((LANES, LANES), np.float32)
    pq = np.zeros((LANES, LANES), np.float32)
    kones = np.zeros((1, LANES), np.float32)
    qones = np.zeros((LANES, TS), np.float32)
    for h in range(FOX_HEADS):
        for i in range(3):
            pk[FOX_HEADS * i + h, BIAS_SLOTS * h + i] = -1.0
            qones[BIAS_SLOTS * h + i, :] = 1.0
            pq[BIAS_SLOTS * h + 3 + i, FOX_HEADS * i + h] = 1.0
            kones[0, BIAS_SLOTS * h + 3 + i] = 1.0
    return (jnp.asarray(pk, BF16), jnp.asarray(kones), jnp.asarray(pq, BF16), jnp.asarray(qones))


def kernel(x, norm1_g, w_in, sgu_ln_g, sgu_ln_b, w_spatial, b_spatial, b_forget, b_gate,
           w_proj_a, w_proj_b, w_out, norm_f_g):
    B, S, D = x.shape
    assert D == D_MODEL and S % TS == 0 and TQ == TK and TS == TK
    H = FOX_HEADS

    w_sgu = w_in[:, :3 * D].astype(BF16)
    w_q, w_k, w_v, w_zb = (w_in[:, (3 + i) * D:(4 + i) * D] for i in range(4))
    w_kz = jnp.concatenate([w_k, w_zb], axis=1).astype(BF16)
    w_f = jnp.pad(w_in[:, 7 * D:7 * D + H], ((0, 0), (0, BF16_ROWS - H)))
    w_qvft = jnp.concatenate([w_q, w_v, w_f], axis=1).T.astype(BF16)
    w_g = w_in[:, 7 * D + H:].astype(BF16)
    b_ft = jnp.broadcast_to(b_forget[:, None], (H, TS))
    row = lambda t: t.reshape(1, -1)
    pk, kones, pq, qones = _bias_slot_constants()

    NS = S // TS
    tok = lambda: pl.BlockSpec((1, TS, D), lambda b, s: (b, s, 0))
    feat = lambda rows: pl.BlockSpec((1, 1, rows, TS), lambda b, s: (b, s, 0, 0))
    tok_out = jax.ShapeDtypeStruct((B, S, D), BF16)
    feat_out = lambda rows: jax.ShapeDtypeStruct((B, NS, rows, TS), BF16)
    qt, qbt, k, kb, vt, zb, gb, ma = pl.pallas_call(
        _proj_kernel,
        grid=(B, NS),
        in_specs=[
            tok(),
            _resident((1, D)), _resident((D, 3 * D)), _resident((D, 2 * D)),
            _resident((2 * D + BF16_ROWS, D)), _resident((D, 2 * D)),
            _resident((1, D)), _resident((1, D)),
            _resident((A_GROUPS, SGU_BLOCK, SGU_BLOCK)), _resident((SGU_BLOCK, A_GROUPS)),
            _resident((H, TS)), _resident((1, 2 * D)), _resident((D, D)),
            _resident((LANES, LANES)), _resident((1, LANES)), _resident((LANES, LANES)),
            _resident((LANES, TS)),
        ],
        out_specs=[feat(D), feat(LANES), tok(),
                   pl.BlockSpec((1, TS, LANES), lambda b, s: (b, s, 0)),
                   feat(D), tok(), tok(), tok()],
        out_shape=[feat_out(D), feat_out(LANES), tok_out,
                   jax.ShapeDtypeStruct((B, S, LANES), BF16),
                   feat_out(D), tok_out, tok_out, tok_out],
        scratch_shapes=[pltpu.VMEM((TS, D), BF16), pltpu.VMEM((H, TS), F32)],
        compiler_params=pltpu.CompilerParams(
            dimension_semantics=("arbitrary", "arbitrary"), vmem_limit_bytes=VMEM_LIMIT_BYTES),
        name="proj_sgu",
    )(x, row(norm1_g), w_sgu, w_kz, w_qvft, w_g, row(sgu_ln_g), row(sgu_ln_b), w_spatial,
      b_spatial.T, b_ft, b_gate.reshape(1, 2 * D), w_proj_a.astype(BF16), pk, kones, pq, qones)

    qtok = lambda: pl.BlockSpec((1, TQ, D), lambda b, s: (b, s, 0))
    per_batch = lambda shape: pl.BlockSpec((1,) + shape, lambda b, s: (b,) + (0,) * len(shape))
    return pl.pallas_call(
        _attn_kernel,
        grid=(B, S // TQ),
        in_specs=[
            feat(D), feat(LANES),
            per_batch((S, D)), per_batch((S, LANES)), per_batch((NS, D, TK)),
            qtok(), qtok(), qtok(), qtok(),
            _resident((D, D)), _resident((D, D)), _resident((1, D)),
        ],
        out_specs=qtok(),
        out_shape=jax.ShapeDtypeStruct((B, S, D), x.dtype),
        scratch_shapes=[pltpu.VMEM((TQ, D), BF16),
                        pltpu.VMEM((H, FOX_HEAD_DIM + LANES, TQ), BF16),
                        pltpu.VMEM((H, 8, TQ), F32),
                        pltpu.VMEM((H, ACC_ROWS, TQ), F32),
                        pltpu.VMEM((H, TK, TQ), F32), pltpu.VMEM((H, TK, TQ), F32)],
        compiler_params=pltpu.CompilerParams(
            dimension_semantics=("arbitrary", "arbitrary"), vmem_limit_bytes=VMEM_LIMIT_BYTES),
        name="fox_merge",
    )(qt, qbt, k, kb, vt, zb, gb, ma, x, w_proj_b.astype(BF16), w_out.astype(BF16),
      row(norm_f_g))
```

```python
import math

import numpy as np
import jax
import jax.numpy as jnp
from jax import lax
from jax.experimental import pallas as pl
from jax.experimental.pallas import tpu as pltpu

D_MODEL = 1024
CHUNK = 64
SGU_BLOCK = 128
A_GROUPS = 8
A_GROUP_DIM = D_MODEL // A_GROUPS
FOX_HEADS = 8
FOX_HEAD_DIM = D_MODEL // FOX_HEADS
EPS = 1e-6
NEG_INF = -1e30
LOG2E = math.log2(math.e)

LANES = 128
BF16_ROWS = 16
TS = 256
TQ = 256
TK = 256
VMEM_LIMIT_BYTES = 60 * 1024 * 1024
BIAS_SLOTS = LANES // FOX_HEADS
ACC_ROWS = FOX_HEAD_DIM + BF16_ROWS

F32 = jnp.float32
BF16 = jnp.bfloat16
_NT = (((1,), (1,)), ((), ()))
_TN = (((0,), (0,)), ((), ()))


def _dot(a, b):
    return jnp.dot(a, b, preferred_element_type=F32)


def _dot_nt(a, b):
    return lax.dot_general(a, b, _NT, preferred_element_type=F32)


def _sigmoid(t):
    return 1.0 / (1.0 + jnp.exp(-t))


def _split3_rows(t):
    hi = t.astype(BF16).astype(F32)
    r1 = t - hi
    mid = r1.astype(BF16).astype(F32)
    lo = (r1 - mid).astype(BF16).astype(F32)
    pad = jnp.zeros((LANES - 3 * t.shape[0], t.shape[1]), F32)
    return jnp.concatenate([hi, mid, lo, pad], axis=0).astype(BF16)


def _proj_kernel(x_ref, g1_ref, wsgu_ref, wkz_ref, wqvft_ref, wg_ref, lng_ref, lnb_ref,
                 ws_ref, bst_ref, bft_ref, bg_ref, wpa_ref, pk_ref, kones_ref, pq_ref, qones_ref,
                 qt_ref, qbt_ref, k_ref, kb_ref, vt_ref, zb_ref, gb_ref, ma_ref,
                 a_scr, carry_scr):
    @pl.when(pl.program_id(1) == 0)
    def _():
        carry_scr[...] = jnp.zeros_like(carry_scr)

    x = x_ref[0]
    ms = jnp.mean(x * x, axis=-1, keepdims=True)
    xn = (x * lax.rsqrt(ms + EPS) * g1_ref[...]).astype(BF16)

    uvz = _dot(xn, wsgu_ref[...])
    pos = lax.broadcasted_iota(jnp.int32, (SGU_BLOCK, SGU_BLOCK), 0) // CHUNK
    src = lax.broadcasted_iota(jnp.int32, (SGU_BLOCK, SGU_BLOCK), 1) // CHUNK
    causal = pos >= src
    for g in range(A_GROUPS):
        c0 = g * A_GROUP_DIM
        ws = jnp.where(causal, ws_ref[g], 0.0).astype(BF16)
        bias = bst_ref[:, g:g + 1]
        lng = lng_ref[:, c0:c0 + A_GROUP_DIM]
        lnb = lnb_ref[:, c0:c0 + A_GROUP_DIM]
        vns = []
        for n in range(TS // SGU_BLOCK):
            r0 = n * SGU_BLOCK
            v = uvz[r0:r0 + SGU_BLOCK, D_MODEL + c0:D_MODEL + c0 + A_GROUP_DIM]
            mu = jnp.mean(v, axis=-1, keepdims=True)
            d = v - mu
            var = jnp.mean(d * d, axis=-1, keepdims=True)
            vns.append(((d * lax.rsqrt(var + EPS)) * lng + lnb).astype(BF16))
        y_all = _dot(ws, jnp.concatenate(vns, axis=1))
        for n in range(TS // SGU_BLOCK):
            r0 = n * SGU_BLOCK
            u = uvz[r0:r0 + SGU_BLOCK, c0:c0 + A_GROUP_DIM]
            z = uvz[r0:r0 + SGU_BLOCK, 2 * D_MODEL + c0:2 * D_MODEL + c0 + A_GROUP_DIM]
            y = y_all[:, n * A_GROUP_DIM:(n + 1) * A_GROUP_DIM] + bias
            a = u * y * (z * _sigmoid(z))
            a_scr[r0:r0 + SGU_BLOCK, c0:c0 + A_GROUP_DIM] = a.astype(BF16)

    gates = _dot(xn, wg_ref[...]) + bg_ref[...]
    ma = _sigmoid(gates[:, :D_MODEL]) * _dot(a_scr[...], wpa_ref[...])
    ma_ref[0] = ma.astype(BF16)
    gb_ref[0] = _sigmoid(gates[:, D_MODEL:]).astype(BF16)

    kz = _dot(xn, wkz_ref[...])
    k_ref[0] = kz[:, :D_MODEL].astype(BF16)
    zb = kz[:, D_MODEL:]
    zb_ref[0] = (zb * _sigmoid(zb)).astype(BF16)
    qvft = _dot_nt(wqvft_ref[...], xn)
    qt_ref[0, 0] = (qvft[:D_MODEL] * (LOG2E / math.sqrt(FOX_HEAD_DIM))).astype(BF16)
    vt_ref[0, 0] = qvft[D_MODEL:2 * D_MODEL].astype(BF16)

    f = qvft[2 * D_MODEL:2 * D_MODEL + FOX_HEADS] + bft_ref[...]
    logf = jnp.minimum(f, 0.0) - jnp.log1p(jnp.exp(-jnp.abs(f)))
    triu = (lax.broadcasted_iota(jnp.int32, (TS, TS), 0)
            <= lax.broadcasted_iota(jnp.int32, (TS, TS), 1)).astype(BF16)
    r = _dot(_split3_rows(logf), triu)
    c = carry_scr[...] + (r[0:FOX_HEADS] + r[FOX_HEADS:2 * FOX_HEADS]
                          + r[2 * FOX_HEADS:3 * FOX_HEADS])
    carry_scr[...] = jnp.broadcast_to(c[:, TS - 1:TS], c.shape)
    parts = _split3_rows(c * LOG2E)
    qb = qones_ref[...] + _dot(pq_ref[...], parts)
    qbt_ref[0, 0] = qb.astype(BF16)
    kb = kones_ref[...] + lax.dot_general(parts, pk_ref[...], _TN, preferred_element_type=F32)
    kb_ref[0] = kb.astype(BF16)


def _attn_kernel(qt_ref, qbt_ref, k_ref, kb_ref, vt_ref, zb_ref, gb_ref, ma_ref, x_ref,
                 wpb_ref, wout_ref, gf_ref, out_ref,
                 o_scr, qa_scr, m_scr, acc_scr, s_scr, p_scr, al_scr, ml_scr):
    qi = pl.program_id(1)
    kpos = lax.broadcasted_iota(jnp.int32, (TK, TQ), 0)
    qpos = lax.broadcasted_iota(jnp.int32, (TK, TQ), 1)
    diag_mask = kpos <= qpos
    slot_head = lax.broadcasted_iota(jnp.int32, (LANES, TQ), 0) // BIAS_SLOTS
    qbt = qbt_ref[0, 0]
    for h in range(FOX_HEADS):
        c0 = h * FOX_HEAD_DIM
        qa_scr[h, :FOX_HEAD_DIM, :] = qt_ref[0, 0, c0:c0 + FOX_HEAD_DIM, :]
        qa_scr[h, FOX_HEAD_DIM:, :] = jnp.where(slot_head == h, qbt, jnp.zeros_like(qbt))
    m_scr[...] = jnp.full(m_scr.shape, NEG_INF, F32)
    acc_scr[...] = jnp.zeros_like(acc_scr)
    ones_rows = jnp.ones((BF16_ROWS, TK), BF16)

    def scores(j, h):
        c0 = h * FOX_HEAD_DIM
        k0 = pl.multiple_of(j * TK, TK)
        ka = jnp.concatenate([k_ref[0, pl.ds(k0, TK), c0:c0 + FOX_HEAD_DIM],
                              kb_ref[0, pl.ds(k0, TK), :]], axis=1)
        return _dot(ka, qa_scr[h])

    def q_op(b, par, h):
        st = scores(b, h)
        s_scr[par, h] = st
        ml_scr[par, h, 0:1, :] = jnp.max(st, axis=0, keepdims=True)

    def x_op(par, h, masked):
        st = s_scr[par, h]
        if masked:
            st = jnp.where(diag_mask, st, NEG_INF)
            m_loc = jnp.max(st, axis=0, keepdims=True)
        else:
            m_loc = ml_scr[par, h, 0:1, :]
        m_old = m_scr[h, 0:1, :]
        m_new = jnp.maximum(m_old, m_loc)
        al_scr[par, h, 0:1, :] = jnp.exp2(m_old - m_new)
        p_scr[par, h] = jnp.exp2(st - m_new).astype(BF16)
        m_scr[h, 0:1, :] = m_new

    def v_op(b, par, h):
        c0 = h * FOX_HEAD_DIM
        va = jnp.concatenate([vt_ref[0, b, c0:c0 + FOX_HEAD_DIM, :], ones_rows], axis=0)
        acc_scr[h] = al_scr[par, h, 0:1, :] * acc_scr[h] + _dot(va, p_scr[par, h])

    def slot(q=None, x=None, v=None):
        for h in range(FOX_HEADS):
            if q is not None:
                q_op(q[0], q[1], h)
            if v is not None:
                v_op(v[0], v[1], h)
            if x is not None:
                x_op(x[0], h, x[1])

    slot(q=(0, 0))

    @pl.when(qi == 0)
    def _():
        slot(x=(0, True))
        slot(v=(0, 0))

    @pl.when(qi >= 1)
    def _():
        slot(q=(1, 1), x=(0, False))

    @pl.loop(0, jnp.maximum(qi - 1, 0) // 2)
    def _(i):
        t = 2 * i
        slot(q=(t + 2, 0), x=(1, False), v=(t, 0))
        slot(q=(t + 3, 1), x=(0, False), v=(t + 1, 1))

    @pl.when((qi >= 2) & (qi % 2 == 0))
    def _():
        slot(q=(qi, 0), x=(1, False), v=(qi - 2, 0))
        slot(x=(0, True), v=(qi - 1, 1))
        slot(v=(qi, 0))

    @pl.when(qi % 2 == 1)
    def _():
        slot(x=(1, True), v=(qi - 1, 0))
        slot(v=(qi, 1))

    for h in range(FOX_HEADS):
        c0 = h * FOX_HEAD_DIM
        acc = acc_scr[h]
        ot = acc[:FOX_HEAD_DIM] / acc[FOX_HEAD_DIM:FOX_HEAD_DIM + 1]
        o = ot.T * zb_ref[0, :, c0:c0 + FOX_HEAD_DIM].astype(F32)
        o_scr[:, c0:c0 + FOX_HEAD_DIM] = o.astype(BF16)

    pb = _dot(o_scr[...], wpb_ref[...])
    merged = gb_ref[0].astype(F32) * pb + ma_ref[0].astype(F32)
    hres = x_ref[0] + _dot(merged.astype(BF16), wout_ref[...])
    ms = jnp.mean(hres * hres, axis=-1, keepdims=True)
    out_ref[0] = hres * lax.rsqrt(ms + EPS) * gf_ref[...]


def _resident(shape):
    return pl.BlockSpec(shape, lambda b, s: (0,) * len(shape), pipeline_mode=pl.Buffered(1))


def _bias_slot_constants():
    pk = np.zeros((LANES, LANES), np.float32)
    pq = np.zeros((LANES, LANES), np.float32)
    kones = np.zeros((1, LANES), np.float32)
    qones = np.zeros((LANES, TS), np.float32)
    for h in range(FOX_HEADS):
        for i in range(3):
            pk[FOX_HEADS * i + h, BIAS_SLOTS * h + i] = -1.0
            qones[BIAS_SLOTS * h + i, :] = 1.0
            pq[BIAS_SLOTS * h + 3 + i, FOX_HEADS * i + h] = 1.0
            kones[0, BIAS_SLOTS * h + 3 + i] = 1.0
    return (jnp.asarray(pk, BF16), jnp.asarray(kones), jnp.asarray(pq, BF16), jnp.asarray(qones))


def kernel(x, norm1_g, w_in, sgu_ln_g, sgu_ln_b, w_spatial, b_spatial, b_forget, b_gate,
           w_proj_a, w_proj_b, w_out, norm_f_g):
    B, S, D = x.shape
    assert D == D_MODEL and S % TS == 0 and TQ == TK and TS == TK
    H = FOX_HEADS

    w_sgu = w_in[:, :3 * D].astype(BF16)
    w_q, w_k, w_v, w_zb = (w_in[:, (3 + i) * D:(4 + i) * D] for i in range(4))
    w_kz = jnp.concatenate([w_k, w_zb], axis=1).astype(BF16)
    w_f = jnp.pad(w_in[:, 7 * D:7 * D + H], ((0, 0), (0, BF16_ROWS - H)))
    w_qvft = jnp.concatenate([w_q, w_v, w_f], axis=1).T.astype(BF16)
    w_g = w_in[:, 7 * D + H:].astype(BF16)
    b_ft = jnp.broadcast_to(b_forget[:, None], (H, TS))
    row = lambda t: t.reshape(1, -1)
    pk, kones, pq, qones = _bias_slot_constants()

    NS = S // TS
    tok = lambda: pl.BlockSpec((1, TS, D), lambda b, s: (b, s, 0))
    feat = lambda rows: pl.BlockSpec((1, 1, rows, TS), lambda b, s: (b, s, 0, 0))
    tok_out = jax.ShapeDtypeStruct((B, S, D), BF16)
    feat_out = lambda rows: jax.ShapeDtypeStruct((B, NS, rows, TS), BF16)
    qt, qbt, k, kb, vt, zb, gb, ma = pl.pallas_call(
        _proj_kernel,
        grid=(B, NS),
        in_specs=[
            tok(),
            _resident((1, D)), _resident((D, 3 * D)), _resident((D, 2 * D)),
            _resident((2 * D + BF16_ROWS, D)), _resident((D, 2 * D)),
            _resident((1, D)), _resident((1, D)),
            _resident((A_GROUPS, SGU_BLOCK, SGU_BLOCK)), _resident((SGU_BLOCK, A_GROUPS)),
            _resident((H, TS)), _resident((1, 2 * D)), _resident((D, D)),
            _resident((LANES, LANES)), _resident((1, LANES)), _resident((LANES, LANES)),
            _resident((LANES, TS)),
        ],
        out_specs=[feat(D), feat(LANES), tok(),
                   pl.BlockSpec((1, TS, LANES), lambda b, s: (b, s, 0)),
                   feat(D), tok(), tok(), tok()],
        out_shape=[feat_out(D), feat_out(LANES), tok_out,
                   jax.ShapeDtypeStruct((B, S, LANES), BF16),
                   feat_out(D), tok_out, tok_out, tok_out],
        scratch_shapes=[pltpu.VMEM((TS, D), BF16), pltpu.VMEM((H, TS), F32)],
        compiler_params=pltpu.CompilerParams(
            dimension_semantics=("arbitrary", "arbitrary"), vmem_limit_bytes=VMEM_LIMIT_BYTES),
        name="proj_sgu",
    )(x, row(norm1_g), w_sgu, w_kz, w_qvft, w_g, row(sgu_ln_g), row(sgu_ln_b), w_spatial,
      b_spatial.T, b_ft, b_gate.reshape(1, 2 * D), w_proj_a.astype(BF16), pk, kones, pq, qones)

    qtok = lambda: pl.BlockSpec((1, TQ, D), lambda b, s: (b, s, 0))
    per_batch = lambda shape: pl.BlockSpec((1,) + shape, lambda b, s: (b,) + (0,) * len(shape))
    return pl.pallas_call(
        _attn_kernel,
        grid=(B, S // TQ),
        in_specs=[
            feat(D), feat(LANES),
            per_batch((S, D)), per_batch((S, LANES)), per_batch((NS, D, TK)),
            qtok(), qtok(), qtok(), qtok(),
            _resident((D, D)), _resident((D, D)), _resident((1, D)),
        ],
        out_specs=qtok(),
        out_shape=jax.ShapeDtypeStruct((B, S, D), x.dtype),
        scratch_shapes=[pltpu.VMEM((TQ, D), BF16),
                        pltpu.VMEM((H, FOX_HEAD_DIM + LANES, TQ), BF16),
                        pltpu.VMEM((H, 8, TQ), F32),
                        pltpu.VMEM((H, ACC_ROWS, TQ), F32),
                        pltpu.VMEM((2, H, TK, TQ), F32), pltpu.VMEM((2, H, TK, TQ), BF16),
                        pltpu.VMEM((2, H, 8, TQ), F32), pltpu.VMEM((2, H, 8, TQ), F32)],
        compiler_params=pltpu.CompilerParams(
            dimension_semantics=("arbitrary", "arbitrary"), vmem_limit_bytes=VMEM_LIMIT_BYTES),
        name="fox_merge",
    )(qt, qbt, k, kb, vt, zb, gb, ma, x, w_proj_b.astype(BF16), w_out.astype(BF16),
      row(norm_f_g))
```

```python
import math

import numpy as np
import jax
import jax.numpy as jnp
from jax import lax
from jax.experimental import pallas as pl
from jax.experimental.pallas import tpu as pltpu

D_MODEL = 1024
CHUNK = 64
SGU_BLOCK = 128
A_GROUPS = 8
A_GROUP_DIM = D_MODEL // A_GROUPS
FOX_HEADS = 8
FOX_HEAD_DIM = D_MODEL // FOX_HEADS
EPS = 1e-6
NEG_INF = -1e30
LOG2E = math.log2(math.e)

LANES = 128
BF16_ROWS = 16
TS = 512
TQ = 256
TK = 256
KEY_BLOCKS_PER_CHUNK = TS // TK
VMEM_LIMIT_BYTES = 60 * 1024 * 1024
BIAS_SLOTS = LANES // FOX_HEADS
ACC_ROWS = FOX_HEAD_DIM + BF16_ROWS

F32 = jnp.float32
BF16 = jnp.bfloat16
_NT = (((1,), (1,)), ((), ()))
_TN = (((0,), (0,)), ((), ()))


def _dot(a, b):
    return jnp.dot(a, b, preferred_element_type=F32)


def _dot_nt(a, b):
    return lax.dot_general(a, b, _NT, preferred_element_type=F32)


def _sigmoid(t):
    return 1.0 / (1.0 + jnp.exp(-t))


def _split3_rows(t):
    hi = t.astype(BF16).astype(F32)
    r1 = t - hi
    mid = r1.astype(BF16).astype(F32)
    lo = (r1 - mid).astype(BF16).astype(F32)
    pad = jnp.zeros((LANES - 3 * t.shape[0], t.shape[1]), F32)
    return jnp.concatenate([hi, mid, lo, pad], axis=0).astype(BF16)


def _proj_kernel(x_ref, g1_ref, wsgu_ref, wkz_ref, wqvft_ref, wg_ref, lng_ref, lnb_ref,
                 ws_ref, bst_ref, bft_ref, bg_ref, wpa_ref, pk_ref, kones_ref, pq_ref, qones_ref,
                 qt_ref, qbt_ref, k_ref, kb_ref, vt_ref, zb_ref, gb_ref, ma_ref,
                 a_scr, carry_scr):
    @pl.when(pl.program_id(1) == 0)
    def _():
        carry_scr[...] = jnp.zeros_like(carry_scr)

    x = x_ref[0]
    ms = jnp.mean(x * x, axis=-1, keepdims=True)
    xn = (x * lax.rsqrt(ms + EPS) * g1_ref[...]).astype(BF16)

    uvz = _dot(xn, wsgu_ref[...])
    pos = lax.broadcasted_iota(jnp.int32, (SGU_BLOCK, SGU_BLOCK), 0) // CHUNK
    src = lax.broadcasted_iota(jnp.int32, (SGU_BLOCK, SGU_BLOCK), 1) // CHUNK
    causal = pos >= src
    for g in range(A_GROUPS):
        c0 = g * A_GROUP_DIM
        ws = jnp.where(causal, ws_ref[g], 0.0).astype(BF16)
        bias = bst_ref[:, g:g + 1]
        lng = lng_ref[:, c0:c0 + A_GROUP_DIM]
        lnb = lnb_ref[:, c0:c0 + A_GROUP_DIM]
        vns = []
        for n in range(TS // SGU_BLOCK):
            r0 = n * SGU_BLOCK
            v = uvz[r0:r0 + SGU_BLOCK, D_MODEL + c0:D_MODEL + c0 + A_GROUP_DIM]
            mu = jnp.mean(v, axis=-1, keepdims=True)
            d = v - mu
            var = jnp.mean(d * d, axis=-1, keepdims=True)
            vns.append(((d * lax.rsqrt(var + EPS)) * lng + lnb).astype(BF16))
        y_all = _dot(ws, jnp.concatenate(vns, axis=1))
        for n in range(TS // SGU_BLOCK):
            r0 = n * SGU_BLOCK
            u = uvz[r0:r0 + SGU_BLOCK, c0:c0 + A_GROUP_DIM]
            z = uvz[r0:r0 + SGU_BLOCK, 2 * D_MODEL + c0:2 * D_MODEL + c0 + A_GROUP_DIM]
            y = y_all[:, n * A_GROUP_DIM:(n + 1) * A_GROUP_DIM] + bias
            a = u * y * (z * _sigmoid(z))
            a_scr[r0:r0 + SGU_BLOCK, c0:c0 + A_GROUP_DIM] = a.astype(BF16)

    gates = _dot(xn, wg_ref[...]) + bg_ref[...]
    ma = _sigmoid(gates[:, :D_MODEL]) * _dot(a_scr[...], wpa_ref[...])
    ma_ref[0] = ma.astype(BF16)
    gb_ref[0] = _sigmoid(gates[:, D_MODEL:]).astype(BF16)

    kz = _dot(xn, wkz_ref[...])
    k_ref[0] = kz[:, :D_MODEL].astype(BF16)
    zb = kz[:, D_MODEL:]
    zb_ref[0] = (zb * _sigmoid(zb)).astype(BF16)
    qvft = _dot_nt(wqvft_ref[...], xn)
    qt_ref[0, 0] = (qvft[:D_MODEL] * (LOG2E / math.sqrt(FOX_HEAD_DIM))).astype(BF16)
    vt_ref[0, 0] = qvft[D_MODEL:2 * D_MODEL].astype(BF16)

    f = qvft[2 * D_MODEL:2 * D_MODEL + FOX_HEADS] + bft_ref[...]
    logf = jnp.minimum(f, 0.0) - jnp.log1p(jnp.exp(-jnp.abs(f)))
    triu = (lax.broadcasted_iota(jnp.int32, (TS, TS), 0)
            <= lax.broadcasted_iota(jnp.int32, (TS, TS), 1)).astype(BF16)
    r = _dot(_split3_rows(logf), triu)
    c = carry_scr[...] + (r[0:FOX_HEADS] + r[FOX_HEADS:2 * FOX_HEADS]
                          + r[2 * FOX_HEADS:3 * FOX_HEADS])
    carry_scr[...] = jnp.broadcast_to(c[:, TS - 1:TS], c.shape)
    parts = _split3_rows(c * LOG2E)
    qb = qones_ref[...] + _dot(pq_ref[...], parts)
    qbt_ref[0, 0] = qb.astype(BF16)
    kb = kones_ref[...] + lax.dot_general(parts, pk_ref[...], _TN, preferred_element_type=F32)
    kb_ref[0] = kb.astype(BF16)


def _attn_kernel(qt_ref, qbt_ref, k_ref, kb_ref, vt_ref, zb_ref, gb_ref, ma_ref, x_ref,
                 wpb_ref, wout_ref, gf_ref, out_ref,
                 o_scr, qa_scr, m_scr, acc_scr, s_scr, p_scr, al_scr, ml_scr):
    qi = pl.program_id(1)
    kpos = lax.broadcasted_iota(jnp.int32, (TK, TQ), 0)
    qpos = lax.broadcasted_iota(jnp.int32, (TK, TQ), 1)
    diag_mask = kpos <= qpos
    slot_head = lax.broadcasted_iota(jnp.int32, (LANES, TQ), 0) // BIAS_SLOTS
    qbt = qbt_ref[0, 0]
    for h in range(FOX_HEADS):
        c0 = h * FOX_HEAD_DIM
        qa_scr[h, :FOX_HEAD_DIM, :] = qt_ref[0, 0, c0:c0 + FOX_HEAD_DIM, :]
        qa_scr[h, FOX_HEAD_DIM:, :] = jnp.where(slot_head == h, qbt, jnp.zeros_like(qbt))
    m_scr[...] = jnp.full(m_scr.shape, NEG_INF, F32)
    acc_scr[...] = jnp.zeros_like(acc_scr)
    ones_rows = jnp.ones((BF16_ROWS, TK), BF16)

    def scores(j, h):
        c0 = h * FOX_HEAD_DIM
        k0 = pl.multiple_of(j * TK, TK)
        ka = jnp.concatenate([k_ref[0, pl.ds(k0, TK), c0:c0 + FOX_HEAD_DIM],
                              kb_ref[0, pl.ds(k0, TK), :]], axis=1)
        return _dot(ka, qa_scr[h])

    def q_op(b, par, h):
        st = scores(b, h)
        s_scr[par, h] = st
        ml_scr[par, h, 0:1, :] = jnp.max(st, axis=0, keepdims=True)

    def x_op(par, h, masked):
        st = s_scr[par, h]
        if masked:
            st = jnp.where(diag_mask, st, NEG_INF)
            m_loc = jnp.max(st, axis=0, keepdims=True)
        else:
            m_loc = ml_scr[par, h, 0:1, :]
        m_old = m_scr[h, 0:1, :]
        m_new = jnp.maximum(m_old, m_loc)
        al_scr[par, h, 0:1, :] = jnp.exp2(m_old - m_new)
        p_scr[par, h] = jnp.exp2(st - m_new).astype(BF16)
        m_scr[h, 0:1, :] = m_new

    def v_op(b, par, h):
        c0 = h * FOX_HEAD_DIM
        vt = vt_ref[0, b // KEY_BLOCKS_PER_CHUNK, c0:c0 + FOX_HEAD_DIM,
                    (par % KEY_BLOCKS_PER_CHUNK) * TK:(par % KEY_BLOCKS_PER_CHUNK + 1) * TK]
        va = jnp.concatenate([vt, ones_rows], axis=0)
        acc_scr[h] = al_scr[par, h, 0:1, :] * acc_scr[h] + _dot(va, p_scr[par, h])

    def slot(q=None, x=None, v=None):
        for h in range(FOX_HEADS):
            if q is not None:
                q_op(q[0], q[1], h)
            if v is not None:
                v_op(v[0], v[1], h)
            if x is not None:
                x_op(x[0], h, x[1])

    slot(q=(0, 0))

    @pl.when(qi == 0)
    def _():
        slot(x=(0, True))
        slot(v=(0, 0))

    @pl.when(qi >= 1)
    def _():
        slot(q=(1, 1), x=(0, False))

    @pl.loop(0, jnp.maximum(qi - 1, 0) // 2)
    def _(i):
        t = 2 * i
        slot(q=(t + 2, 0), x=(1, False), v=(t, 0))
        slot(q=(t + 3, 1), x=(0, False), v=(t + 1, 1))

    @pl.when((qi >= 2) & (qi % 2 == 0))
    def _():
        slot(q=(qi, 0), x=(1, False), v=(qi - 2, 0))
        slot(x=(0, True), v=(qi - 1, 1))
        slot(v=(qi, 0))

    @pl.when(qi % 2 == 1)
    def _():
        slot(x=(1, True), v=(qi - 1, 0))
        slot(v=(qi, 1))

    for h in range(FOX_HEADS):
        c0 = h * FOX_HEAD_DIM
        acc = acc_scr[h]
        ot = acc[:FOX_HEAD_DIM] / acc[FOX_HEAD_DIM:FOX_HEAD_DIM + 1]
        o = ot.T * zb_ref[0, :, c0:c0 + FOX_HEAD_DIM].astype(F32)
        o_scr[:, c0:c0 + FOX_HEAD_DIM] = o.astype(BF16)

    pb = _dot(o_scr[...], wpb_ref[...])
    merged = gb_ref[0].astype(F32) * pb + ma_ref[0].astype(F32)
    hres = x_ref[0] + _dot(merged.astype(BF16), wout_ref[...])
    ms = jnp.mean(hres * hres, axis=-1, keepdims=True)
    out_ref[0] = hres * lax.rsqrt(ms + EPS) * gf_ref[...]


def _resident(shape):
    return pl.BlockSpec(shape, lambda b, s: (0,) * len(shape), pipeline_mode=pl.Buffered(1))


def _bias_slot_constants():
    pk = np.zeros((LANES, LANES), np.float32)
    pq = np.zeros((LANES, LANES), np.float32)
    kones = np.zeros((1, LANES), np.float32)
    qones = np.zeros((LANES, TS), np.float32)
    for h in range(FOX_HEADS):
        for i in range(3):
            pk[FOX_HEADS * i + h, BIAS_SLOTS * h + i] = -1.0
            qones[BIAS_SLOTS * h + i, :] = 1.0
            pq[BIAS_SLOTS * h + 3 + i, FOX_HEADS * i + h] = 1.0
            kones[0, BIAS_SLOTS * h + 3 + i] = 1.0
    return (jnp.asarray(pk, BF16), jnp.asarray(kones), jnp.asarray(pq, BF16), jnp.asarray(qones))


def kernel(x, norm1_g, w_in, sgu_ln_g, sgu_ln_b, w_spatial, b_spatial, b_forget, b_gate,
           w_proj_a, w_proj_b, w_out, norm_f_g):
    B, S, D = x.shape
    assert D == D_MODEL and S % TS == 0 and TQ == TK and KEY_BLOCKS_PER_CHUNK in (1, 2)
    H = FOX_HEADS

    w_sgu = w_in[:, :3 * D].astype(BF16)
    w_q, w_k, w_v, w_zb = (w_in[:, (3 + i) * D:(4 + i) * D] for i in range(4))
    w_kz = jnp.concatenate([w_k, w_zb], axis=1).astype(BF16)
    w_f = jnp.pad(w_in[:, 7 * D:7 * D + H], ((0, 0), (0, BF16_ROWS - H)))
    w_qvft = jnp.concatenate([w_q, w_v, w_f], axis=1).T.astype(BF16)
    w_g = w_in[:, 7 * D + H:].astype(BF16)
    b_ft = jnp.broadcast_to(b_forget[:, None], (H, TS))
    row = lambda t: t.reshape(1, -1)
    pk, kones, pq, qones = _bias_slot_constants()

    NS = S // TS
    tok = lambda: pl.BlockSpec((1, TS, D), lambda b, s: (b, s, 0))
    feat = lambda rows: pl.BlockSpec((1, 1, rows, TS), lambda b, s: (b, s, 0, 0))
    tok_out = jax.ShapeDtypeStruct((B, S, D), BF16)
    feat_out = lambda rows: jax.ShapeDtypeStruct((B, NS, rows, TS), BF16)
    qt, qbt, k, kb, vt, zb, gb, ma = pl.pallas_call(
        _proj_kernel,
        grid=(B, NS),
        in_specs=[
            tok(),
            _resident((1, D)), _resident((D, 3 * D)), _resident((D, 2 * D)),
            _resident((2 * D + BF16_ROWS, D)), _resident((D, 2 * D)),
            _resident((1, D)), _resident((1, D)),
            _resident((A_GROUPS, SGU_BLOCK, SGU_BLOCK)), _resident((SGU_BLOCK, A_GROUPS)),
            _resident((H, TS)), _resident((1, 2 * D)), _resident((D, D)),
            _resident((LANES, LANES)), _resident((1, LANES)), _resident((LANES, LANES)),
            _resident((LANES, TS)),
        ],
        out_specs=[feat(D), feat(LANES), tok(),
                   pl.BlockSpec((1, TS, LANES), lambda b, s: (b, s, 0)),
                   feat(D), tok(), tok(), tok()],
        out_shape=[feat_out(D), feat_out(LANES), tok_out,
                   jax.ShapeDtypeStruct((B, S, LANES), BF16),
                   feat_out(D), tok_out, tok_out, tok_out],
        scratch_shapes=[pltpu.VMEM((TS, D), BF16), pltpu.VMEM((H, TS), F32)],
        compiler_params=pltpu.CompilerParams(
            dimension_semantics=("arbitrary", "arbitrary"), vmem_limit_bytes=VMEM_LIMIT_BYTES),
        name="proj_sgu",
    )(x, row(norm1_g), w_sgu, w_kz, w_qvft, w_g, row(sgu_ln_g), row(sgu_ln_b), w_spatial,
      b_spatial.T, b_ft, b_gate.reshape(1, 2 * D), w_proj_a.astype(BF16), pk, kones, pq, qones)

    qtok = lambda: pl.BlockSpec((1, TQ, D), lambda b, s: (b, s, 0))
    qfeat = lambda rows: pl.BlockSpec(
        (1, 1, rows, TQ),
        lambda b, s: (b, s // KEY_BLOCKS_PER_CHUNK, 0, s % KEY_BLOCKS_PER_CHUNK))
    per_batch = lambda shape: pl.BlockSpec((1,) + shape, lambda b, s: (b,) + (0,) * len(shape))
    return pl.pallas_call(
        _attn_kernel,
        grid=(B, S // TQ),
        in_specs=[
            qfeat(D), qfeat(LANES),
            per_batch((S, D)), per_batch((S, LANES)), per_batch((NS, D, TS)),
            qtok(), qtok(), qtok(), qtok(),
            _resident((D, D)), _resident((D, D)), _resident((1, D)),
        ],
        out_specs=qtok(),
        out_shape=jax.ShapeDtypeStruct((B, S, D), x.dtype),
        scratch_shapes=[pltpu.VMEM((TQ, D), BF16),
                        pltpu.VMEM((H, FOX_HEAD_DIM + LANES, TQ), BF16),
                        pltpu.VMEM((H, 8, TQ), F32),
                        pltpu.VMEM((H, ACC_ROWS, TQ), F32),
                        pltpu.VMEM((2, H, TK, TQ), F32), pltpu.VMEM((2, H, TK, TQ), BF16),
                        pltpu.VMEM((2, H, 8, TQ), F32), pltpu.VMEM((2, H, 8, TQ), F32)],
        compiler_params=pltpu.CompilerParams(
            dimension_semantics=("arbitrary", "arbitrary"), vmem_limit_bytes=VMEM_LIMIT_BYTES),
        name="fox_merge",
    )(qt, qbt, k, kb, vt, zb, gb, ma, x, w_proj_b.astype(BF16), w_out.astype(BF16),
      row(norm_f_g))
```

```python
import math

import numpy as np
import jax
import jax.numpy as jnp
from jax import lax
from jax.experimental import pallas as pl
from jax.experimental.pallas import tpu as pltpu

D_MODEL = 1024
CHUNK = 64
SGU_BLOCK = 128
A_GROUPS = 8
A_GROUP_DIM = D_MODEL // A_GROUPS
FOX_HEADS = 8
FOX_HEAD_DIM = D_MODEL // FOX_HEADS
EPS = 1e-6
NEG_INF = -1e30
LOG2E = math.log2(math.e)

LANES = 128
BF16_ROWS = 16
TS = 512
TQ = 256
TK = 256
KEY_BLOCKS_PER_CHUNK = TS // TK
VMEM_LIMIT_BYTES = 60 * 1024 * 1024
BIAS_SLOTS = LANES // FOX_HEADS
ACC_ROWS = FOX_HEAD_DIM + BF16_ROWS

F32 = jnp.float32
BF16 = jnp.bfloat16
_NT = (((1,), (1,)), ((), ()))
_TN = (((0,), (0,)), ((), ()))


def _dot(a, b):
    return jnp.dot(a, b, preferred_element_type=F32)


def _dot_nt(a, b):
    return lax.dot_general(a, b, _NT, preferred_element_type=F32)


def _sigmoid(t):
    return 1.0 / (1.0 + jnp.exp(-t))


def _split3_rows(t):
    hi = t.astype(BF16).astype(F32)
    r1 = t - hi
    mid = r1.astype(BF16).astype(F32)
    lo = (r1 - mid).astype(BF16).astype(F32)
    pad = jnp.zeros((LANES - 3 * t.shape[0], t.shape[1]), F32)
    return jnp.concatenate([hi, mid, lo, pad], axis=0).astype(BF16)


def _proj_kernel(x_ref, g1_ref, wsgu_ref, wkz_ref, wqvft_ref, wg_ref, lng_ref, lnb_ref,
                 ws_ref, bst_ref, bft_ref, bg_ref, wpa_ref, pk_ref, kones_ref, pq_ref, qones_ref,
                 qt_ref, qbt_ref, k_ref, kb_ref, vt_ref, zb_ref, gb_ref, ma_ref,
                 a_scr, carry_scr):
    @pl.when(pl.program_id(1) == 0)
    def _():
        carry_scr[...] = jnp.zeros_like(carry_scr)

    x = x_ref[0]
    ms = jnp.mean(x * x, axis=-1, keepdims=True)
    xn = (x * lax.rsqrt(ms + EPS) * g1_ref[...]).astype(BF16)

    uvz = _dot(xn, wsgu_ref[...])
    pos = lax.broadcasted_iota(jnp.int32, (SGU_BLOCK, SGU_BLOCK), 0) // CHUNK
    src = lax.broadcasted_iota(jnp.int32, (SGU_BLOCK, SGU_BLOCK), 1) // CHUNK
    causal = pos >= src
    for g in range(A_GROUPS):
        c0 = g * A_GROUP_DIM
        ws = jnp.where(causal, ws_ref[g], 0.0).astype(BF16)
        bias = bst_ref[:, g:g + 1]
        lng = lng_ref[:, c0:c0 + A_GROUP_DIM]
        lnb = lnb_ref[:, c0:c0 + A_GROUP_DIM]
        vns = []
        for n in range(TS // SGU_BLOCK):
            r0 = n * SGU_BLOCK
            v = uvz[r0:r0 + SGU_BLOCK, D_MODEL + c0:D_MODEL + c0 + A_GROUP_DIM]
            mu = jnp.mean(v, axis=-1, keepdims=True)
            d = v - mu
            var = jnp.mean(d * d, axis=-1, keepdims=True)
            vns.append(((d * lax.rsqrt(var + EPS)) * lng + lnb).astype(BF16))
        y_all = _dot(ws, jnp.concatenate(vns, axis=1))
        for n in range(TS // SGU_BLOCK):
            r0 = n * SGU_BLOCK
            u = uvz[r0:r0 + SGU_BLOCK, c0:c0 + A_GROUP_DIM]
            z = uvz[r0:r0 + SGU_BLOCK, 2 * D_MODEL + c0:2 * D_MODEL + c0 + A_GROUP_DIM]
            y = y_all[:, n * A_GROUP_DIM:(n + 1) * A_GROUP_DIM] + bias
            a = u * y * (z * _sigmoid(z))
            a_scr[r0:r0 + SGU_BLOCK, c0:c0 + A_GROUP_DIM] = a.astype(BF16)

    gates = _dot(xn, wg_ref[...]) + bg_ref[...]
    ma = _sigmoid(gates[:, :D_MODEL]) * _dot(a_scr[...], wpa_ref[...])
    ma_ref[0] = ma.astype(BF16)
    gb_ref[0] = _sigmoid(gates[:, D_MODEL:]).astype(BF16)

    kz = _dot(xn, wkz_ref[...])
    k_ref[0] = kz[:, :D_MODEL].astype(BF16)
    zb = kz[:, D_MODEL:]
    zb_ref[0] = (zb * _sigmoid(zb)).astype(BF16)
    qvft = lax.dot_general(wqvft_ref[...], xn, (((0,), (1,)), ((), ())),
                           preferred_element_type=F32)
    qt_ref[0, 0] = (qvft[:D_MODEL] * (LOG2E / math.sqrt(FOX_HEAD_DIM))).astype(BF16)
    vt_ref[0, 0] = qvft[D_MODEL:2 * D_MODEL].astype(BF16)

    f = qvft[2 * D_MODEL:2 * D_MODEL + FOX_HEADS] + bft_ref[...]
    logf = jnp.minimum(f, 0.0) - jnp.log1p(jnp.exp(-jnp.abs(f)))
    triu = (lax.broadcasted_iota(jnp.int32, (TS, TS), 0)
            <= lax.broadcasted_iota(jnp.int32, (TS, TS), 1)).astype(BF16)
    r = _dot(_split3_rows(logf), triu)
    c = carry_scr[...] + (r[0:FOX_HEADS] + r[FOX_HEADS:2 * FOX_HEADS]
                          + r[2 * FOX_HEADS:3 * FOX_HEADS])
    carry_scr[...] = jnp.broadcast_to(c[:, TS - 1:TS], c.shape)
    parts = _split3_rows(c * LOG2E)
    qb = qones_ref[...] + _dot(pq_ref[...], parts)
    qbt_ref[0, 0] = qb.astype(BF16)
    kb = kones_ref[...] + lax.dot_general(parts, pk_ref[...], _TN, preferred_element_type=F32)
    kb_ref[0] = kb.astype(BF16)


def _attn_kernel(qt_ref, qbt_ref, k_ref, kb_ref, vt_ref, zb_ref, gb_ref, ma_ref, x_ref,
                 wpb_ref, wout_ref, gf_ref, out_ref,
                 o_scr, qa_scr, m_scr, acc_scr, s_scr, p_scr, al_scr, ml_scr):
    qi = pl.program_id(1)
    kpos = lax.broadcasted_iota(jnp.int32, (TK, TQ), 0)
    qpos = lax.broadcasted_iota(jnp.int32, (TK, TQ), 1)
    diag_mask = kpos <= qpos
    slot_head = lax.broadcasted_iota(jnp.int32, (LANES, TQ), 0) // BIAS_SLOTS
    qbt = qbt_ref[0, 0]
    for h in range(FOX_HEADS):
        c0 = h * FOX_HEAD_DIM
        qa_scr[h, :FOX_HEAD_DIM, :] = qt_ref[0, 0, c0:c0 + FOX_HEAD_DIM, :]
        qa_scr[h, FOX_HEAD_DIM:, :] = jnp.where(slot_head == h, qbt, jnp.zeros_like(qbt))
    m_scr[...] = jnp.full(m_scr.shape, NEG_INF, F32)
    acc_scr[...] = jnp.zeros_like(acc_scr)
    ones_rows = jnp.ones((BF16_ROWS, TK), BF16)

    def scores(j, h):
        c0 = h * FOX_HEAD_DIM
        k0 = pl.multiple_of(j * TK, TK)
        ka = jnp.concatenate([k_ref[0, pl.ds(k0, TK), c0:c0 + FOX_HEAD_DIM],
                              kb_ref[0, pl.ds(k0, TK), :]], axis=1)
        return _dot(ka, qa_scr[h])

    def q_op(b, par, h):
        st = scores(b, h)
        s_scr[par, h] = st
        ml_scr[par, h, 0:1, :] = jnp.max(st, axis=0, keepdims=True)

    def x_op(par, h, masked):
        st = s_scr[par, h]
        if masked:
            st = jnp.where(diag_mask, st, NEG_INF)
            m_loc = jnp.max(st, axis=0, keepdims=True)
        else:
            m_loc = ml_scr[par, h, 0:1, :]
        m_old = m_scr[h, 0:1, :]
        m_new = jnp.maximum(m_old, m_loc)
        al_scr[par, h, 0:1, :] = jnp.exp2(m_old - m_new)
        p_scr[par, h] = jnp.exp2(st - m_new).astype(BF16)
        m_scr[h, 0:1, :] = m_new

    def v_op(b, par, h):
        c0 = h * FOX_HEAD_DIM
        vt = vt_ref[0, b // KEY_BLOCKS_PER_CHUNK, c0:c0 + FOX_HEAD_DIM,
                    (par % KEY_BLOCKS_PER_CHUNK) * TK:(par % KEY_BLOCKS_PER_CHUNK + 1) * TK]
        va = jnp.concatenate([vt, ones_rows], axis=0)
        acc_scr[h] = al_scr[par, h, 0:1, :] * acc_scr[h] + _dot(va, p_scr[par, h])

    def slot(q=None, x=None, v=None):
        for h in range(FOX_HEADS):
            if q is not None:
                q_op(q[0], q[1], h)
            if v is not None:
                v_op(v[0], v[1], h)
            if x is not None:
                x_op(x[0], h, x[1])

    slot(q=(0, 0))

    @pl.when(qi == 0)
    def _():
        slot(x=(0, True))
        slot(v=(0, 0))

    @pl.when(qi >= 1)
    def _():
        slot(q=(1, 1), x=(0, False))

    @pl.loop(0, jnp.maximum(qi - 1, 0) // 2)
    def _(i):
        t = 2 * i
        slot(q=(t + 2, 0), x=(1, False), v=(t, 0))
        slot(q=(t + 3, 1), x=(0, False), v=(t + 1, 1))

    @pl.when((qi >= 2) & (qi % 2 == 0))
    def _():
        slot(q=(qi, 0), x=(1, False), v=(qi - 2, 0))
        slot(x=(0, True), v=(qi - 1, 1))
        slot(v=(qi, 0))

    @pl.when(qi % 2 == 1)
    def _():
        slot(x=(1, True), v=(qi - 1, 0))
        slot(v=(qi, 1))

    for h in range(FOX_HEADS):
        c0 = h * FOX_HEAD_DIM
        acc = acc_scr[h]
        ot = acc[:FOX_HEAD_DIM] / acc[FOX_HEAD_DIM:FOX_HEAD_DIM + 1]
        o = ot.T * zb_ref[0, :, c0:c0 + FOX_HEAD_DIM].astype(F32)
        o_scr[:, c0:c0 + FOX_HEAD_DIM] = o.astype(BF16)

    pb = _dot(o_scr[...], wpb_ref[...])
    merged = gb_ref[0].astype(F32) * pb + ma_ref[0].astype(F32)
    hres = x_ref[0] + _dot(merged.astype(BF16), wout_ref[...])
    ms = jnp.mean(hres * hres, axis=-1, keepdims=True)
    out_ref[0] = hres * lax.rsqrt(ms + EPS) * gf_ref[...]


def _resident(shape):
    return pl.BlockSpec(shape, lambda b, s: (0,) * len(shape), pipeline_mode=pl.Buffered(1))


def _bias_slot_constants():
    pk = np.zeros((LANES, LANES), np.float32)
    pq = np.zeros((LANES, LANES), np.float32)
    kones = np.zeros((1, LANES), np.float32)
    qones = np.zeros((LANES, TS), np.float32)
    for h in range(FOX_HEADS):
        for i in range(3):
            pk[FOX_HEADS * i + h, BIAS_SLOTS * h + i] = -1.0
            qones[BIAS_SLOTS * h + i, :] = 1.0
            pq[BIAS_SLOTS * h + 3 + i, FOX_HEADS * i + h] = 1.0
            kones[0, BIAS_SLOTS * h + 3 + i] = 1.0
    return (jnp.asarray(pk, BF16), jnp.asarray(kones), jnp.asarray(pq, BF16), jnp.asarray(qones))


def kernel(x, norm1_g, w_in, sgu_ln_g, sgu_ln_b, w_spatial, b_spatial, b_forget, b_gate,
           w_proj_a, w_proj_b, w_out, norm_f_g):
    B, S, D = x.shape
    assert D == D_MODEL and S % TS == 0 and TQ == TK and KEY_BLOCKS_PER_CHUNK in (1, 2)
    H = FOX_HEADS

    w_sgu = w_in[:, :3 * D].astype(BF16)
    w_q, w_k, w_v, w_zb = (w_in[:, (3 + i) * D:(4 + i) * D] for i in range(4))
    w_kz = jnp.concatenate([w_k, w_zb], axis=1).astype(BF16)
    w_f = jnp.pad(w_in[:, 7 * D:7 * D + H], ((0, 0), (0, BF16_ROWS - H)))
    w_qvft = jnp.concatenate([w_q, w_v, w_f], axis=1).astype(BF16)
    w_g = w_in[:, 7 * D + H:].astype(BF16)
    b_ft = jnp.broadcast_to(b_forget[:, None], (H, TS))
    row = lambda t: t.reshape(1, -1)
    pk, kones, pq, qones = _bias_slot_constants()

    NS = S // TS
    tok = lambda: pl.BlockSpec((1, TS, D), lambda b, s: (b, s, 0))
    feat = lambda rows: pl.BlockSpec((1, 1, rows, TS), lambda b, s: (b, s, 0, 0))
    tok_out = jax.ShapeDtypeStruct((B, S, D), BF16)
    feat_out = lambda rows: jax.ShapeDtypeStruct((B, NS, rows, TS), BF16)
    qt, qbt, k, kb, vt, zb, gb, ma = pl.pallas_call(
        _proj_kernel,
        grid=(B, NS),
        in_specs=[
            tok(),
            _resident((1, D)), _resident((D, 3 * D)), _resident((D, 2 * D)),
            _resident((D, 2 * D + BF16_ROWS)), _resident((D, 2 * D)),
            _resident((1, D)), _resident((1, D)),
            _resident((A_GROUPS, SGU_BLOCK, SGU_BLOCK)), _resident((SGU_BLOCK, A_GROUPS)),
            _resident((H, TS)), _resident((1, 2 * D)), _resident((D, D)),
            _resident((LANES, LANES)), _resident((1, LANES)), _resident((LANES, LANES)),
            _resident((LANES, TS)),
        ],
        out_specs=[feat(D), feat(LANES), tok(),
                   pl.BlockSpec((1, TS, LANES), lambda b, s: (b, s, 0)),
                   feat(D), tok(), tok(), tok()],
        out_shape=[feat_out(D), feat_out(LANES), tok_out,
                   jax.ShapeDtypeStruct((B, S, LANES), BF16),
                   feat_out(D), tok_out, tok_out, tok_out],
        scratch_shapes=[pltpu.VMEM((TS, D), BF16), pltpu.VMEM((H, TS), F32)],
        compiler_params=pltpu.CompilerParams(
            dimension_semantics=("arbitrary", "arbitrary"), vmem_limit_bytes=VMEM_LIMIT_BYTES),
        name="proj_sgu",
    )(x, row(norm1_g), w_sgu, w_kz, w_qvft, w_g, row(sgu_ln_g), row(sgu_ln_b), w_spatial,
      b_spatial.T, b_ft, b_gate.reshape(1, 2 * D), w_proj_a.astype(BF16), pk, kones, pq, qones)

    qtok = lambda: pl.BlockSpec((1, TQ, D), lambda b, s: (b, s, 0))
    qfeat = lambda rows: pl.BlockSpec(
        (1, 1, rows, TQ),
        lambda b, s: (b, s // KEY_BLOCKS_PER_CHUNK, 0, s % KEY_BLOCKS_PER_CHUNK))
    per_batch = lambda shape: pl.BlockSpec((1,) + shape, lambda b, s: (b,) + (0,) * len(shape))
    return pl.pallas_call(
        _attn_kernel,
        grid=(B, S // TQ),
        in_specs=[
            qfeat(D), qfeat(LANES),
            per_batch((S, D)), per_batch((S, LANES)), per_batch((NS, D, TS)),
            qtok(), qtok(), qtok(), qtok(),
            _resident((D, D)), _resident((D, D)), _resident((1, D)),
        ],
        out_specs=qtok(),
        out_shape=jax.ShapeDtypeStruct((B, S, D), x.dtype),
        scratch_shapes=[pltpu.VMEM((TQ, D), BF16),
                        pltpu.VMEM((H, FOX_HEAD_DIM + LANES, TQ), BF16),
                        pltpu.VMEM((H, 8, TQ), F32),
                        pltpu.VMEM((H, ACC_ROWS, TQ), F32),
                        pltpu.VMEM((2, H, TK, TQ), F32), pltpu.VMEM((2, H, TK, TQ), BF16),
                        pltpu.VMEM((2, H, 8, TQ), F32), pltpu.VMEM((2, H, 8, TQ), F32)],
        compiler_params=pltpu.CompilerParams(
            dimension_semantics=("arbitrary", "arbitrary"), vmem_limit_bytes=VMEM_LIMIT_BYTES),
        name="fox_merge",
    )(qt, qbt, k, kb, vt, zb, gb, ma, x, w_proj_b.astype(BF16), w_out.astype(BF16),
      row(norm_f_g))
```

```python
import math

import numpy as np
import jax
import jax.numpy as jnp
from jax import lax
from jax.experimental import pallas as pl
from jax.experimental.pallas import tpu as pltpu

D_MODEL = 1024
CHUNK = 64
SGU_BLOCK = 128
A_GROUPS = 8
A_GROUP_DIM = D_MODEL // A_GROUPS
FOX_HEADS = 8
FOX_HEAD_DIM = D_MODEL // FOX_HEADS
EPS = 1e-6
NEG_INF = -1e30
LOG2E = math.log2(math.e)

LANES = 128
BF16_ROWS = 16
TS = 512
TQ = 256
TK = 256
KEY_BLOCKS_PER_CHUNK = TS // TK
VMEM_LIMIT_BYTES = 60 * 1024 * 1024
BIAS_SLOTS = LANES // FOX_HEADS
ACC_ROWS = FOX_HEAD_DIM + BF16_ROWS

F32 = jnp.float32
BF16 = jnp.bfloat16


def _dot(a, b):
    return jnp.dot(a, b, preferred_element_type=F32)


def _dot_tn(w, t):
    return lax.dot_general(w, t, (((0,), (1,)), ((), ())), preferred_element_type=F32)


def _dot_tt(a, b):
    return lax.dot_general(a, b, (((0,), (0,)), ((), ())), preferred_element_type=F32)


def _sigmoid(t):
    return 1.0 / (1.0 + jnp.exp(-t))


def _split3_rows(t):
    hi = t.astype(BF16).astype(F32)
    r1 = t - hi
    mid = r1.astype(BF16).astype(F32)
    lo = (r1 - mid).astype(BF16).astype(F32)
    pad = jnp.zeros((LANES - 3 * t.shape[0], t.shape[1]), F32)
    return jnp.concatenate([hi, mid, lo, pad], axis=0).astype(BF16)


def _proj_kernel(x_ref, g1_ref, wsgu_ref, wk_ref, wzb_ref, wqvf_ref, wg_ref,
                 lng_ref, lnb_ref,
                 ws_ref, bst_ref, bft_ref, bg_ref, wpa_ref, pk_ref, kones_ref, pq_ref, qones_ref,
                 qt_ref, qbt_ref, k_ref, kb_ref, vt_ref, zb_ref, gb_ref, ma_ref,
                 a_scr, carry_scr):
    @pl.when(pl.program_id(1) == 0)
    def _():
        carry_scr[...] = jnp.zeros_like(carry_scr)

    x = x_ref[0]
    ms = jnp.mean(x * x, axis=-1, keepdims=True)
    xn = (x * lax.rsqrt(ms + EPS) * g1_ref[...]).astype(BF16)

    uvz = _dot(xn, wsgu_ref[...])
    pos = lax.broadcasted_iota(jnp.int32, (SGU_BLOCK, SGU_BLOCK), 0) // CHUNK
    src = lax.broadcasted_iota(jnp.int32, (SGU_BLOCK, SGU_BLOCK), 1) // CHUNK
    causal = pos >= src
    for g in range(A_GROUPS):
        c0 = g * A_GROUP_DIM
        ws = jnp.where(causal, ws_ref[g], 0.0).astype(BF16)
        bias = bst_ref[:, g:g + 1]
        lng = lng_ref[:, c0:c0 + A_GROUP_DIM]
        lnb = lnb_ref[:, c0:c0 + A_GROUP_DIM]
        vns = []
        for n in range(TS // SGU_BLOCK):
            r0 = n * SGU_BLOCK
            v = uvz[r0:r0 + SGU_BLOCK, D_MODEL + c0:D_MODEL + c0 + A_GROUP_DIM]
            mu = jnp.mean(v, axis=-1, keepdims=True)
            d = v - mu
            var = jnp.mean(d * d, axis=-1, keepdims=True)
            vns.append(((d * lax.rsqrt(var + EPS)) * lng + lnb).astype(BF16))
        y_all = _dot(ws, jnp.concatenate(vns, axis=1))
        for n in range(TS // SGU_BLOCK):
            r0 = n * SGU_BLOCK
            u = uvz[r0:r0 + SGU_BLOCK, c0:c0 + A_GROUP_DIM]
            z = uvz[r0:r0 + SGU_BLOCK, 2 * D_MODEL + c0:2 * D_MODEL + c0 + A_GROUP_DIM]
            y = y_all[:, n * A_GROUP_DIM:(n + 1) * A_GROUP_DIM] + bias
            a = u * y * (z * _sigmoid(z))
            a_scr[r0:r0 + SGU_BLOCK, c0:c0 + A_GROUP_DIM] = a.astype(BF16)

    gates = _dot(xn, wg_ref[...]) + bg_ref[...]
    ma = _sigmoid(gates[:, :D_MODEL]) * _dot(a_scr[...], wpa_ref[...])
    ma_ref[0] = ma.astype(BF16)
    gb_ref[0] = _sigmoid(gates[:, D_MODEL:]).astype(BF16)

    k_ref[0] = _dot(xn, wk_ref[...]).astype(BF16)
    zb = _dot(xn, wzb_ref[...])
    zb_ref[0] = (zb * _sigmoid(zb)).astype(BF16)
    qvft = _dot_tn(wqvf_ref[...], xn)
    qt_ref[0, 0] = (qvft[:D_MODEL] * (LOG2E / math.sqrt(FOX_HEAD_DIM))).astype(BF16)
    vt_ref[0, 0] = qvft[D_MODEL:2 * D_MODEL].astype(BF16)

    f = qvft[2 * D_MODEL:2 * D_MODEL + FOX_HEADS] + bft_ref[...]
    logf = jnp.minimum(f, 0.0) - jnp.log1p(jnp.exp(-jnp.abs(f)))
    triu = (lax.broadcasted_iota(jnp.int32, (TS, TS), 0)
            <= lax.broadcasted_iota(jnp.int32, (TS, TS), 1)).astype(BF16)
    r = _dot(_split3_rows(logf), triu)
    c = carry_scr[...] + (r[0:FOX_HEADS] + r[FOX_HEADS:2 * FOX_HEADS]
                          + r[2 * FOX_HEADS:3 * FOX_HEADS])
    carry_scr[...] = jnp.broadcast_to(c[:, TS - 1:TS], c.shape)
    parts = _split3_rows(c * LOG2E)
    qb = qones_ref[...] + _dot(pq_ref[...], parts)
    qbt_ref[0, 0] = qb.astype(BF16)
    kb = kones_ref[...] + _dot_tt(parts, pk_ref[...])
    kb_ref[0] = kb.astype(BF16)


def _attn_kernel(qt_ref, qbt_ref, k_ref, kb_ref, vt_ref, zb_ref, gb_ref, ma_ref, x_ref,
                 wpb_ref, wout_ref, gf_ref, out_ref,
                 o_scr, qa_scr, m_scr, acc_scr, s_scr, p_scr, al_scr, ml_scr):
    qi = pl.program_id(1)
    kpos = lax.broadcasted_iota(jnp.int32, (TK, TQ), 0)
    qpos = lax.broadcasted_iota(jnp.int32, (TK, TQ), 1)
    diag_mask = kpos <= qpos
    slot_head = lax.broadcasted_iota(jnp.int32, (LANES, TQ), 0) // BIAS_SLOTS
    qbt = qbt_ref[0, 0]
    for h in range(FOX_HEADS):
        c0 = h * FOX_HEAD_DIM
        qa_scr[h, :FOX_HEAD_DIM, :] = qt_ref[0, 0, c0:c0 + FOX_HEAD_DIM, :]
        qa_scr[h, FOX_HEAD_DIM:, :] = jnp.where(slot_head == h, qbt, jnp.zeros_like(qbt))
    m_scr[...] = jnp.full(m_scr.shape, NEG_INF, F32)
    acc_scr[...] = jnp.zeros_like(acc_scr)
    ones_rows = jnp.ones((BF16_ROWS, TK), BF16)

    def scores(j, h):
        c0 = h * FOX_HEAD_DIM
        k0 = pl.multiple_of(j * TK, TK)
        ka = jnp.concatenate([k_ref[0, pl.ds(k0, TK), c0:c0 + FOX_HEAD_DIM],
                              kb_ref[0, pl.ds(k0, TK), :]], axis=1)
        return _dot(ka, qa_scr[h])

    def q_op(b, par, h):
        st = scores(b, h)
        s_scr[par, h] = st
        ml_scr[par, h, 0:1, :] = jnp.max(st, axis=0, keepdims=True)

    def x_op(par, h, masked):
        st = s_scr[par, h]
        if masked:
            st = jnp.where(diag_mask, st, NEG_INF)
            m_loc = jnp.max(st, axis=0, keepdims=True)
        else:
            m_loc = ml_scr[par, h, 0:1, :]
        m_old = m_scr[h, 0:1, :]
        m_new = jnp.maximum(m_old, m_loc)
        al_scr[par, h, 0:1, :] = jnp.exp2(m_old - m_new)
        p_scr[par, h] = jnp.exp2(st - m_new).astype(BF16)
        m_scr[h, 0:1, :] = m_new

    def v_op(b, par, h):
        c0 = h * FOX_HEAD_DIM
        vt = vt_ref[0, b // KEY_BLOCKS_PER_CHUNK, c0:c0 + FOX_HEAD_DIM,
                    (par % KEY_BLOCKS_PER_CHUNK) * TK:(par % KEY_BLOCKS_PER_CHUNK + 1) * TK]
        va = jnp.concatenate([vt, ones_rows], axis=0)
        acc_scr[h] = al_scr[par, h, 0:1, :] * acc_scr[h] + _dot(va, p_scr[par, h])

    def slot(q=None, x=None, v=None):
        for h in range(FOX_HEADS):
            if q is not None:
                q_op(q[0], q[1], h)
            if v is not None:
                v_op(v[0], v[1], h)
            if x is not None:
                x_op(x[0], h, x[1])

    slot(q=(0, 0))

    @pl.when(qi == 0)
    def _():
        slot(x=(0, True))
        slot(v=(0, 0))

    @pl.when(qi >= 1)
    def _():
        slot(q=(1, 1), x=(0, False))

    @pl.loop(0, jnp.maximum(qi - 1, 0) // 2)
    def _(i):
        t = 2 * i
        slot(q=(t + 2, 0), x=(1, False), v=(t, 0))
        slot(q=(t + 3, 1), x=(0, False), v=(t + 1, 1))

    @pl.when((qi >= 2) & (qi % 2 == 0))
    def _():
        slot(q=(qi, 0), x=(1, False), v=(qi - 2, 0))
        slot(x=(0, True), v=(qi - 1, 1))
        slot(v=(qi, 0))

    @pl.when(qi % 2 == 1)
    def _():
        slot(x=(1, True), v=(qi - 1, 0))
        slot(v=(qi, 1))

    for h in range(FOX_HEADS):
        c0 = h * FOX_HEAD_DIM
        acc = acc_scr[h]
        ot = acc[:FOX_HEAD_DIM] / acc[FOX_HEAD_DIM:FOX_HEAD_DIM + 1]
        o = ot.T * zb_ref[0, :, c0:c0 + FOX_HEAD_DIM].astype(F32)
        o_scr[:, c0:c0 + FOX_HEAD_DIM] = o.astype(BF16)

    pb = _dot(o_scr[...], wpb_ref[...])
    merged = gb_ref[0].astype(F32) * pb + ma_ref[0].astype(F32)
    hres = x_ref[0] + _dot(merged.astype(BF16), wout_ref[...])
    ms = jnp.mean(hres * hres, axis=-1, keepdims=True)
    out_ref[0] = hres * lax.rsqrt(ms + EPS) * gf_ref[...]


def _resident(shape):
    return pl.BlockSpec(shape, lambda b, s: (0,) * len(shape), pipeline_mode=pl.Buffered(1))


def _bias_slot_constants():
    pk = np.zeros((LANES, LANES), np.float32)
    pq = np.zeros((LANES, LANES), np.float32)
    kones = np.zeros((1, LANES), np.float32)
    qones = np.zeros((LANES, TS), np.float32)
    for h in range(FOX_HEADS):
        for i in range(3):
            pk[FOX_HEADS * i + h, BIAS_SLOTS * h + i] = -1.0
            qones[BIAS_SLOTS * h + i, :] = 1.0
            pq[BIAS_SLOTS * h + 3 + i, FOX_HEADS * i + h] = 1.0
            kones[0, BIAS_SLOTS * h + 3 + i] = 1.0
    return (jnp.asarray(pk, BF16), jnp.asarray(kones), jnp.asarray(pq, BF16), jnp.asarray(qones))


def kernel(x, norm1_g, w_in, sgu_ln_g, sgu_ln_b, w_spatial, b_spatial, b_forget, b_gate,
           w_proj_a, w_proj_b, w_out, norm_f_g):
    B, S, D = x.shape
    assert D == D_MODEL and S % TS == 0 and TQ == TK and KEY_BLOCKS_PER_CHUNK in (1, 2)
    H = FOX_HEADS

    w_bf = w_in.astype(BF16)
    w_cols = lambda width, idx: pl.BlockSpec((D, width), lambda b, s: (0, idx),
                                             pipeline_mode=pl.Buffered(1))
    w_f = jnp.pad(w_bf[:, 7 * D:7 * D + H], ((0, 0), (0, BF16_ROWS - H)))
    w_qvf = jnp.concatenate([w_bf[:, 3 * D:4 * D], w_bf[:, 5 * D:6 * D], w_f], axis=1)
    w_g = w_bf[:, 7 * D + H:]
    b_ft = jnp.broadcast_to(b_forget[:, None], (H, TS))
    row = lambda t: t.reshape(1, -1)
    pk, kones, pq, qones = _bias_slot_constants()

    NS = S // TS
    tok = lambda: pl.BlockSpec((1, TS, D), lambda b, s: (b, s, 0))
    feat = lambda rows: pl.BlockSpec((1, 1, rows, TS), lambda b, s: (b, s, 0, 0))
    tok_out = jax.ShapeDtypeStruct((B, S, D), BF16)
    feat_out = lambda rows: jax.ShapeDtypeStruct((B, NS, rows, TS), BF16)
    qt, qbt, k, kb, vt, zb, gb, ma = pl.pallas_call(
        _proj_kernel,
        grid=(B, NS),
        in_specs=[
            tok(),
            _resident((1, D)), w_cols(3 * D, 0), w_cols(D, 4), w_cols(D, 6),
            _resident((D, 2 * D + BF16_ROWS)), _resident((D, 2 * D)),
            _resident((1, D)), _resident((1, D)),
            _resident((A_GROUPS, SGU_BLOCK, SGU_BLOCK)), _resident((SGU_BLOCK, A_GROUPS)),
            _resident((H, TS)), _resident((1, 2 * D)), _resident((D, D)),
            _resident((LANES, LANES)), _resident((1, LANES)), _resident((LANES, LANES)),
            _resident((LANES, TS)),
        ],
        out_specs=[feat(D), feat(LANES), tok(),
                   pl.BlockSpec((1, TS, LANES), lambda b, s: (b, s, 0)),
                   feat(D), tok(), tok(), tok()],
        out_shape=[feat_out(D), feat_out(LANES), tok_out,
                   jax.ShapeDtypeStruct((B, S, LANES), BF16),
                   feat_out(D), tok_out, tok_out, tok_out],
        scratch_shapes=[pltpu.VMEM((TS, D), BF16), pltpu.VMEM((H, TS), F32)],
        compiler_params=pltpu.CompilerParams(
            dimension_semantics=("arbitrary", "arbitrary"), vmem_limit_bytes=VMEM_LIMIT_BYTES),
        name="proj_sgu",
    )(x, row(norm1_g), w_bf, w_bf, w_bf, w_qvf, w_g,
      row(sgu_ln_g), row(sgu_ln_b), w_spatial,
      b_spatial.T, b_ft, b_gate.reshape(1, 2 * D), w_proj_a.astype(BF16), pk, kones, pq, qones)

    qtok = lambda: pl.BlockSpec((1, TQ, D), lambda b, s: (b, s, 0))
    qfeat = lambda rows: pl.BlockSpec(
        (1, 1, rows, TQ),
        lambda b, s: (b, s // KEY_BLOCKS_PER_CHUNK, 0, s % KEY_BLOCKS_PER_CHUNK))
    per_batch = lambda shape: pl.BlockSpec((1,) + shape, lambda b, s: (b,) + (0,) * len(shape))
    return pl.pallas_call(
        _attn_kernel,
        grid=(B, S // TQ),
        in_specs=[
            qfeat(D), qfeat(LANES),
            per_batch((S, D)), per_batch((S, LANES)), per_batch((NS, D, TS)),
            qtok(), qtok(), qtok(), qtok(),
            _resident((D, D)), _resident((D, D)), _resident((1, D)),
        ],
        out_specs=qtok(),
        out_shape=jax.ShapeDtypeStruct((B, S, D), x.dtype),
        scratch_shapes=[pltpu.VMEM((TQ, D), BF16),
                        pltpu.VMEM((H, FOX_HEAD_DIM + LANES, TQ), BF16),
                        pltpu.VMEM((H, 8, TQ), F32),
                        pltpu.VMEM((H, ACC_ROWS, TQ), F32),
                        pltpu.VMEM((2, H, TK, TQ), F32), pltpu.VMEM((2, H, TK, TQ), BF16),
                        pltpu.VMEM((2, H, 8, TQ), F32), pltpu.VMEM((2, H, 8, TQ), F32)],
        compiler_params=pltpu.CompilerParams(
            dimension_semantics=("arbitrary", "arbitrary"), vmem_limit_bytes=VMEM_LIMIT_BYTES),
        name="fox_merge",
    )(qt, qbt, k, kb, vt, zb, gb, ma, x, w_proj_b.astype(BF16), w_out.astype(BF16),
      row(norm_f_g))
```

```python
import math

import numpy as np
import jax
import jax.numpy as jnp
from jax import lax
from jax.experimental import pallas as pl
from jax.experimental.pallas import tpu as pltpu

D_MODEL = 1024
CHUNK = 64
SGU_BLOCK = 128
A_GROUPS = 8
A_GROUP_DIM = D_MODEL // A_GROUPS
FOX_HEADS = 8
FOX_HEAD_DIM = D_MODEL // FOX_HEADS
EPS = 1e-6
NEG_INF = -1e30
LOG2E = math.log2(math.e)

LANES = 128
BF16_ROWS = 16
TS = 512
TQ = 256
TK = 256
KEY_BLOCKS_PER_CHUNK = TS // TK
VMEM_LIMIT_BYTES = 60 * 1024 * 1024
BIAS_SLOTS = LANES // FOX_HEADS
ACC_ROWS = FOX_HEAD_DIM + BF16_ROWS

F32 = jnp.float32
BF16 = jnp.bfloat16


def _dot(a, b):
    return jnp.dot(a, b, preferred_element_type=F32)


def _dot_tn(w, t):
    return lax.dot_general(w, t, (((0,), (1,)), ((), ())), preferred_element_type=F32)


def _dot_tt(a, b):
    return lax.dot_general(a, b, (((0,), (0,)), ((), ())), preferred_element_type=F32)


def _sigmoid(t):
    return 1.0 / (1.0 + jnp.exp(-t))


def _split3_rows(t):
    hi = t.astype(BF16).astype(F32)
    r1 = t - hi
    mid = r1.astype(BF16).astype(F32)
    lo = (r1 - mid).astype(BF16).astype(F32)
    pad = jnp.zeros((LANES - 3 * t.shape[0], t.shape[1]), F32)
    return jnp.concatenate([hi, mid, lo, pad], axis=0).astype(BF16)


def _proj_kernel(x_ref, g1_ref, wsgu_ref, wk_ref, wzb_ref, wqvf_ref, wg_ref,
                 lng_ref, lnb_ref,
                 ws_ref, bst_ref, bft_ref, bg_ref, wpa_ref, pk_ref, kones_ref, pq_ref, qones_ref,
                 qt_ref, qbt_ref, k_ref, kb_ref, vt_ref, zb_ref, gb_ref, ma_ref,
                 a_scr, carry_scr):
    @pl.when(pl.program_id(1) == 0)
    def _():
        carry_scr[...] = jnp.zeros_like(carry_scr)

    x = x_ref[0]
    ms = jnp.mean(x * x, axis=-1, keepdims=True)
    xn = (x * lax.rsqrt(ms + EPS) * g1_ref[...]).astype(BF16)

    uvz = _dot(xn, wsgu_ref[...])
    pos = lax.broadcasted_iota(jnp.int32, (SGU_BLOCK, SGU_BLOCK), 0) // CHUNK
    src = lax.broadcasted_iota(jnp.int32, (SGU_BLOCK, SGU_BLOCK), 1) // CHUNK
    causal = pos >= src
    for g in range(A_GROUPS):
        c0 = g * A_GROUP_DIM
        ws = jnp.where(causal, ws_ref[g], 0.0).astype(BF16)
        bias = bst_ref[:, g:g + 1]
        lng = lng_ref[:, c0:c0 + A_GROUP_DIM]
        lnb = lnb_ref[:, c0:c0 + A_GROUP_DIM]
        vns = []
        for n in range(TS // SGU_BLOCK):
            r0 = n * SGU_BLOCK
            v = uvz[r0:r0 + SGU_BLOCK, D_MODEL + c0:D_MODEL + c0 + A_GROUP_DIM]
            mu = jnp.mean(v, axis=-1, keepdims=True)
            d = v - mu
            var = jnp.mean(d * d, axis=-1, keepdims=True)
            vns.append(((d * lax.rsqrt(var + EPS)) * lng + lnb).astype(BF16))
        y_all = _dot(ws, jnp.concatenate(vns, axis=1))
        for n in range(TS // SGU_BLOCK):
            r0 = n * SGU_BLOCK
            u = uvz[r0:r0 + SGU_BLOCK, c0:c0 + A_GROUP_DIM]
            z = uvz[r0:r0 + SGU_BLOCK, 2 * D_MODEL + c0:2 * D_MODEL + c0 + A_GROUP_DIM]
            y = y_all[:, n * A_GROUP_DIM:(n + 1) * A_GROUP_DIM] + bias
            a = u * y * (z * _sigmoid(z))
            a_scr[r0:r0 + SGU_BLOCK, c0:c0 + A_GROUP_DIM] = a.astype(BF16)

    gates = _dot(xn, wg_ref[...]) + bg_ref[...]
    ma = _sigmoid(gates[:, :D_MODEL]) * _dot(a_scr[...], wpa_ref[...])
    ma_ref[0] = ma.astype(BF16)
    gb_ref[0] = _sigmoid(gates[:, D_MODEL:]).astype(BF16)

    k_ref[0] = _dot(xn, wk_ref[...]).astype(BF16)
    zb = _dot(xn, wzb_ref[...])
    zb_ref[0] = (zb * _sigmoid(zb)).astype(BF16)
    qvft = _dot_tn(wqvf_ref[...], xn)
    qt_ref[0, 0] = (qvft[:D_MODEL] * (LOG2E / math.sqrt(FOX_HEAD_DIM))).astype(BF16)
    vt_ref[0, 0] = qvft[D_MODEL:2 * D_MODEL].astype(BF16)

    f = qvft[2 * D_MODEL:2 * D_MODEL + FOX_HEADS] + bft_ref[...]
    logf = jnp.minimum(f, 0.0) - jnp.log1p(jnp.exp(-jnp.abs(f)))
    triu = (lax.broadcasted_iota(jnp.int32, (TS, TS), 0)
            <= lax.broadcasted_iota(jnp.int32, (TS, TS), 1)).astype(BF16)
    r = _dot(_split3_rows(logf), triu)
    c = carry_scr[...] + (r[0:FOX_HEADS] + r[FOX_HEADS:2 * FOX_HEADS]
                          + r[2 * FOX_HEADS:3 * FOX_HEADS])
    carry_scr[...] = jnp.broadcast_to(c[:, TS - 1:TS], c.shape)
    parts = _split3_rows(c * LOG2E)
    qb = qones_ref[...] + _dot(pq_ref[...], parts)
    qbt_ref[0, 0] = qb.astype(BF16)
    kb = kones_ref[...] + _dot_tt(parts, pk_ref[...])
    kb_ref[0] = kb.astype(BF16)


def _attn_kernel(qt_ref, qbt_ref, k_ref, kb_ref, vt_ref, zb_ref, gb_ref, ma_ref, x_ref,
                 wpb_ref, wout_ref, gf_ref, out_ref,
                 o_scr, qa_scr, m_scr, acc_scr, s_scr, p_scr, al_scr, ml_scr):
    qi = pl.program_id(1)
    kpos = lax.broadcasted_iota(jnp.int32, (TK, TQ), 0)
    qpos = lax.broadcasted_iota(jnp.int32, (TK, TQ), 1)
    diag_mask = kpos <= qpos
    slot_head = lax.broadcasted_iota(jnp.int32, (LANES, TQ), 0) // BIAS_SLOTS
    qbt = qbt_ref[0, 0]
    for h in range(FOX_HEADS):
        c0 = h * FOX_HEAD_DIM
        qa_scr[h, :FOX_HEAD_DIM, :] = qt_ref[0, 0, c0:c0 + FOX_HEAD_DIM, :]
        qa_scr[h, FOX_HEAD_DIM:, :] = jnp.where(slot_head == h, qbt, jnp.zeros_like(qbt))
    m_scr[...] = jnp.full(m_scr.shape, NEG_INF, F32)
    acc_scr[...] = jnp.zeros_like(acc_scr)
    ones_rows = jnp.ones((BF16_ROWS, TK), BF16)

    def scores(j, h):
        c0 = h * FOX_HEAD_DIM
        k0 = pl.multiple_of(j * TK, TK)
        ka = jnp.concatenate([k_ref[0, pl.ds(k0, TK), c0:c0 + FOX_HEAD_DIM],
                              kb_ref[0, pl.ds(k0, TK), :]], axis=1)
        return _dot(ka, qa_scr[h])

    def q_op(b, par, h):
        st = scores(b, h)
        s_scr[par, h] = st
        ml_scr[par, h, 0:1, :] = jnp.max(st, axis=0, keepdims=True)

    def x_op(par, h, masked):
        st = s_scr[par, h]
        if masked:
            st = jnp.where(diag_mask, st, NEG_INF)
            m_loc = jnp.max(st, axis=0, keepdims=True)
        else:
            m_loc = ml_scr[par, h, 0:1, :]
        m_old = m_scr[h, 0:1, :]
        m_new = jnp.maximum(m_old, m_loc)
        al_scr[par, h, 0:1, :] = jnp.exp2(m_old - m_new)
        p_scr[par, h] = jnp.exp2(st - m_new).astype(BF16)
        m_scr[h, 0:1, :] = m_new

    def v_op(b, par, h):
        c0 = h * FOX_HEAD_DIM
        vt = vt_ref[0, b // KEY_BLOCKS_PER_CHUNK, c0:c0 + FOX_HEAD_DIM,
                    (par % KEY_BLOCKS_PER_CHUNK) * TK:(par % KEY_BLOCKS_PER_CHUNK + 1) * TK]
        va = jnp.concatenate([vt, ones_rows], axis=0)
        acc_scr[h] = al_scr[par, h, 0:1, :] * acc_scr[h] + _dot(va, p_scr[par, h])

    def slot(q=None, x=None, v=None):
        for h in range(FOX_HEADS):
            if q is not None:
                q_op(q[0], q[1], h)
            if v is not None:
                v_op(v[0], v[1], h)
            if x is not None:
                x_op(x[0], h, x[1])

    def xv_op(b, par, h, masked):
        c0 = h * FOX_HEAD_DIM
        st = s_scr[par, h]
        if masked:
            st = jnp.where(diag_mask, st, NEG_INF)
            m_loc = jnp.max(st, axis=0, keepdims=True)
        else:
            m_loc = ml_scr[par, h, 0:1, :]
        m_old = m_scr[h, 0:1, :]
        m_new = jnp.maximum(m_old, m_loc)
        alpha = jnp.exp2(m_old - m_new)
        pt = jnp.exp2(st - m_new).astype(BF16)
        m_scr[h, 0:1, :] = m_new
        vt = vt_ref[0, b // KEY_BLOCKS_PER_CHUNK, c0:c0 + FOX_HEAD_DIM,
                    (par % KEY_BLOCKS_PER_CHUNK) * TK:(par % KEY_BLOCKS_PER_CHUNK + 1) * TK]
        va = jnp.concatenate([vt, ones_rows], axis=0)
        acc_scr[h] = alpha * acc_scr[h] + _dot(va, pt)

    def slot2(q=None, xv=None):
        for h in range(FOX_HEADS):
            if q is not None:
                q_op(q[0], q[1], h)
        for h in range(FOX_HEADS):
            if xv is not None:
                xv_op(xv[0], xv[1], h, xv[2])

    slot2(q=(0, 0))

    @pl.loop(0, qi // 2)
    def _(i):
        t = 2 * i
        slot2(q=(t + 1, 1), xv=(t, 0, False))
        slot2(q=(t + 2, 0), xv=(t + 1, 1, False))

    @pl.when(qi % 2 == 0)
    def _():
        slot2(xv=(qi, 0, True))

    @pl.when(qi % 2 == 1)
    def _():
        slot2(q=(qi, 1), xv=(qi - 1, 0, False))
        slot2(xv=(qi, 1, True))

    for h in range(FOX_HEADS):
        c0 = h * FOX_HEAD_DIM
        acc = acc_scr[h]
        ot = acc[:FOX_HEAD_DIM] / acc[FOX_HEAD_DIM:FOX_HEAD_DIM + 1]
        o = ot.T * zb_ref[0, :, c0:c0 + FOX_HEAD_DIM].astype(F32)
        o_scr[:, c0:c0 + FOX_HEAD_DIM] = o.astype(BF16)

    pb = _dot(o_scr[...], wpb_ref[...])
    merged = gb_ref[0].astype(F32) * pb + ma_ref[0].astype(F32)
    hres = x_ref[0] + _dot(merged.astype(BF16), wout_ref[...])
    ms = jnp.mean(hres * hres, axis=-1, keepdims=True)
    out_ref[0] = hres * lax.rsqrt(ms + EPS) * gf_ref[...]


def _resident(shape):
    return pl.BlockSpec(shape, lambda b, s: (0,) * len(shape), pipeline_mode=pl.Buffered(1))


def _bias_slot_constants():
    pk = np.zeros((LANES, LANES), np.float32)
    pq = np.zeros((LANES, LANES), np.float32)
    kones = np.zeros((1, LANES), np.float32)
    qones = np.zeros((LANES, TS), np.float32)
    for h in range(FOX_HEADS):
        for i in range(3):
            pk[FOX_HEADS * i + h, BIAS_SLOTS * h + i] = -1.0
            qones[BIAS_SLOTS * h + i, :] = 1.0
            pq[BIAS_SLOTS * h + 3 + i, FOX_HEADS * i + h] = 1.0
            kones[0, BIAS_SLOTS * h + 3 + i] = 1.0
    return (jnp.asarray(pk, BF16), jnp.asarray(kones), jnp.asarray(pq, BF16), jnp.asarray(qones))


def kernel(x, norm1_g, w_in, sgu_ln_g, sgu_ln_b, w_spatial, b_spatial, b_forget, b_gate,
           w_proj_a, w_proj_b, w_out, norm_f_g):
    B, S, D = x.shape
    assert D == D_MODEL and S % TS == 0 and TQ == TK and KEY_BLOCKS_PER_CHUNK in (1, 2)
    H = FOX_HEADS

    w_bf = w_in.astype(BF16)
    w_cols = lambda width, idx: pl.BlockSpec((D, width), lambda b, s: (0, idx),
                                             pipeline_mode=pl.Buffered(1))
    w_f = jnp.pad(w_bf[:, 7 * D:7 * D + H], ((0, 0), (0, BF16_ROWS - H)))
    w_qvf = jnp.concatenate([w_bf[:, 3 * D:4 * D], w_bf[:, 5 * D:6 * D], w_f], axis=1)
    w_g = w_bf[:, 7 * D + H:]
    b_ft = jnp.broadcast_to(b_forget[:, None], (H, TS))
    row = lambda t: t.reshape(1, -1)
    pk, kones, pq, qones = _bias_slot_constants()

    NS = S // TS
    tok = lambda: pl.BlockSpec((1, TS, D), lambda b, s: (b, s, 0))
    feat = lambda rows: pl.BlockSpec((1, 1, rows, TS), lambda b, s: (b, s, 0, 0))
    tok_out = jax.ShapeDtypeStruct((B, S, D), BF16)
    feat_out = lambda rows: jax.ShapeDtypeStruct((B, NS, rows, TS), BF16)
    qt, qbt, k, kb, vt, zb, gb, ma = pl.pallas_call(
        _proj_kernel,
        grid=(B, NS),
        in_specs=[
            tok(),
            _resident((1, D)), w_cols(3 * D, 0), w_cols(D, 4), w_cols(D, 6),
            _resident((D, 2 * D + BF16_ROWS)), _resident((D, 2 * D)),
            _resident((1, D)), _resident((1, D)),
            _resident((A_GROUPS, SGU_BLOCK, SGU_BLOCK)), _resident((SGU_BLOCK, A_GROUPS)),
            _resident((H, TS)), _resident((1, 2 * D)), _resident((D, D)),
            _resident((LANES, LANES)), _resident((1, LANES)), _resident((LANES, LANES)),
            _resident((LANES, TS)),
        ],
        out_specs=[feat(D), feat(LANES), tok(),
                   pl.BlockSpec((1, TS, LANES), lambda b, s: (b, s, 0)),
                   feat(D), tok(), tok(), tok()],
        out_shape=[feat_out(D), feat_out(LANES), tok_out,
                   jax.ShapeDtypeStruct((B, S, LANES), BF16),
                   feat_out(D), tok_out, tok_out, tok_out],
        scratch_shapes=[pltpu.VMEM((TS, D), BF16), pltpu.VMEM((H, TS), F32)],
        compiler_params=pltpu.CompilerParams(
            dimension_semantics=("arbitrary", "arbitrary"), vmem_limit_bytes=VMEM_LIMIT_BYTES),
        name="proj_sgu",
    )(x, row(norm1_g), w_bf, w_bf, w_bf, w_qvf, w_g,
      row(sgu_ln_g), row(sgu_ln_b), w_spatial,
      b_spatial.T, b_ft, b_gate.reshape(1, 2 * D), w_proj_a.astype(BF16), pk, kones, pq, qones)

    qtok = lambda: pl.BlockSpec((1, TQ, D), lambda b, s: (b, s, 0))
    qfeat = lambda rows: pl.BlockSpec(
        (1, 1, rows, TQ),
        lambda b, s: (b, s // KEY_BLOCKS_PER_CHUNK, 0, s % KEY_BLOCKS_PER_CHUNK))
    per_batch = lambda shape: pl.BlockSpec((1,) + shape, lambda b, s: (b,) + (0,) * len(shape))
    return pl.pallas_call(
        _attn_kernel,
        grid=(B, S // TQ),
        in_specs=[
            qfeat(D), qfeat(LANES),
            per_batch((S, D)), per_batch((S, LANES)), per_batch((NS, D, TS)),
            qtok(), qtok(), qtok(), qtok(),
            _resident((D, D)), _resident((D, D)), _resident((1, D)),
        ],
        out_specs=qtok(),
        out_shape=jax.ShapeDtypeStruct((B, S, D), x.dtype),
        scratch_shapes=[pltpu.VMEM((TQ, D), BF16),
                        pltpu.VMEM((H, FOX_HEAD_DIM + LANES, TQ), BF16),
                        pltpu.VMEM((H, 8, TQ), F32),
                        pltpu.VMEM((H, ACC_ROWS, TQ), F32),
                        pltpu.VMEM((2, H, TK, TQ), F32), pltpu.VMEM((2, H, TK, TQ), BF16),
                        pltpu.VMEM((2, H, 8, TQ), F32), pltpu.VMEM((2, H, 8, TQ), F32)],
        compiler_params=pltpu.CompilerParams(
            dimension_semantics=("arbitrary", "arbitrary"), vmem_limit_bytes=VMEM_LIMIT_BYTES),
        name="fox_merge",
    )(qt, qbt, k, kb, vt, zb, gb, ma, x, w_proj_b.astype(BF16), w_out.astype(BF16),
      row(norm_f_g))
```

```python
import math

import numpy as np
import jax
import jax.numpy as jnp
from jax import lax
from jax.experimental import pallas as pl
from jax.experimental.pallas import tpu as pltpu

D_MODEL = 1024
CHUNK = 64
SGU_BLOCK = 128
A_GROUPS = 8
A_GROUP_DIM = D_MODEL // A_GROUPS
FOX_HEADS = 8
FOX_HEAD_DIM = D_MODEL // FOX_HEADS
EPS = 1e-6
NEG_INF = -1e30
LOG2E = math.log2(math.e)

LANES = 128
BF16_ROWS = 16
TS = 512
TQ = 256
TK = 256
KEY_BLOCKS_PER_CHUNK = TS // TK
VMEM_LIMIT_BYTES = 60000 * 1024
BIAS_SLOTS = LANES // FOX_HEADS
ACC_ROWS = FOX_HEAD_DIM + BF16_ROWS

F32 = jnp.float32
BF16 = jnp.bfloat16


def _dot(a, b):
    return jnp.dot(a, b, preferred_element_type=F32)


def _dot_tn(w, t):
    return lax.dot_general(w, t, (((0,), (1,)), ((), ())), preferred_element_type=F32)


def _dot_tt(a, b):
    return lax.dot_general(a, b, (((0,), (0,)), ((), ())), preferred_element_type=F32)


def _sigmoid(t):
    return 1.0 / (1.0 + jnp.exp(-t))


def _split3_rows(t):
    hi = t.astype(BF16).astype(F32)
    r1 = t - hi
    mid = r1.astype(BF16).astype(F32)
    lo = (r1 - mid).astype(BF16).astype(F32)
    pad = jnp.zeros((LANES - 3 * t.shape[0], t.shape[1]), F32)
    return jnp.concatenate([hi, mid, lo, pad], axis=0).astype(BF16)


def _proj_kernel(x_ref, g1_ref, wsgu_ref, wk_ref, wzb_ref, wqvf_ref, wg_ref,
                 lng_ref, lnb_ref,
                 ws_ref, bst_ref, bft_ref, bg_ref, wpa_ref, pk_ref, kones_ref, pq_ref, qones_ref,
                 qt_ref, qbt_ref, k_ref, kb_ref, vt_ref, zb_ref, gb_ref, ma_ref,
                 a_scr, carry_scr):
    @pl.when(pl.program_id(1) == 0)
    def _():
        carry_scr[...] = jnp.zeros_like(carry_scr)

    x = x_ref[0]
    ms = jnp.mean(x * x, axis=-1, keepdims=True)
    xn = (x * lax.rsqrt(ms + EPS) * g1_ref[...]).astype(BF16)

    uvz = _dot(xn, wsgu_ref[...])
    pos = lax.broadcasted_iota(jnp.int32, (SGU_BLOCK, SGU_BLOCK), 0) // CHUNK
    src = lax.broadcasted_iota(jnp.int32, (SGU_BLOCK, SGU_BLOCK), 1) // CHUNK
    causal = pos >= src
    for g in range(A_GROUPS):
        c0 = g * A_GROUP_DIM
        ws = jnp.where(causal, ws_ref[g], 0.0).astype(BF16)
        bias = bst_ref[:, g:g + 1]
        lng = lng_ref[:, c0:c0 + A_GROUP_DIM]
        lnb = lnb_ref[:, c0:c0 + A_GROUP_DIM]
        vns = []
        for n in range(TS // SGU_BLOCK):
            r0 = n * SGU_BLOCK
            v = uvz[r0:r0 + SGU_BLOCK, D_MODEL + c0:D_MODEL + c0 + A_GROUP_DIM]
            mu = jnp.mean(v, axis=-1, keepdims=True)
            d = v - mu
            var = jnp.mean(d * d, axis=-1, keepdims=True)
            vns.append(((d * lax.rsqrt(var + EPS)) * lng + lnb).astype(BF16))
        y_all = _dot(ws, jnp.concatenate(vns, axis=1))
        for n in range(TS // SGU_BLOCK):
            r0 = n * SGU_BLOCK
            u = uvz[r0:r0 + SGU_BLOCK, c0:c0 + A_GROUP_DIM]
            z = uvz[r0:r0 + SGU_BLOCK, 2 * D_MODEL + c0:2 * D_MODEL + c0 + A_GROUP_DIM]
            y = y_all[:, n * A_GROUP_DIM:(n + 1) * A_GROUP_DIM] + bias
            a = u * y * (z * _sigmoid(z))
            a_scr[r0:r0 + SGU_BLOCK, c0:c0 + A_GROUP_DIM] = a.astype(BF16)

    gates = _dot(xn, wg_ref[...]) + bg_ref[...]
    ma = _sigmoid(gates[:, :D_MODEL]) * _dot(a_scr[...], wpa_ref[...])
    ma_ref[0] = ma.astype(BF16)
    gb_ref[0] = _sigmoid(gates[:, D_MODEL:]).astype(BF16)

    k_ref[0] = _dot(xn, wk_ref[...]).astype(BF16)
    zb = _dot(xn, wzb_ref[...])
    zb_ref[0] = (zb * _sigmoid(zb)).astype(BF16)
    qvft = _dot_tn(wqvf_ref[...], xn)
    qt_ref[0, 0] = (qvft[:D_MODEL] * (LOG2E / math.sqrt(FOX_HEAD_DIM))).astype(BF16)
    vt_ref[0, 0] = qvft[D_MODEL:2 * D_MODEL].astype(BF16)

    f = qvft[2 * D_MODEL:2 * D_MODEL + FOX_HEADS] + bft_ref[...]
    logf = jnp.minimum(f, 0.0) - jnp.log1p(jnp.exp(-jnp.abs(f)))
    triu = (lax.broadcasted_iota(jnp.int32, (TS, TS), 0)
            <= lax.broadcasted_iota(jnp.int32, (TS, TS), 1)).astype(BF16)
    r = _dot(_split3_rows(logf), triu)
    c = carry_scr[...] + (r[0:FOX_HEADS] + r[FOX_HEADS:2 * FOX_HEADS]
                          + r[2 * FOX_HEADS:3 * FOX_HEADS])
    carry_scr[...] = jnp.broadcast_to(c[:, TS - 1:TS], c.shape)
    parts = _split3_rows(c * LOG2E)
    qb = qones_ref[...] + _dot(pq_ref[...], parts)
    qbt_ref[0, 0] = qb.astype(BF16)
    kb = kones_ref[...] + _dot_tt(parts, pk_ref[...])
    kb_ref[0] = kb.astype(BF16)


def _attn_kernel(qt_ref, qbt_ref, k_ref, kb_ref, vt_ref, zb_ref, gb_ref, ma_ref, x_ref,
                 wpb_ref, wout_ref, gf_ref, out_ref,
                 o_scr, qb_scr, m_scr, acc_scr, s_scr, ml_scr):
    qi = pl.program_id(1)
    kpos = {nk: lax.broadcasted_iota(jnp.int32, (nk, TQ), 0) for nk in (TK, 2 * TK)}
    qpos = {nk: lax.broadcasted_iota(jnp.int32, (nk, TQ), 1) for nk in (TK, 2 * TK)}
    diag_mask = {"single": kpos[TK] <= qpos[TK], "pair": kpos[2 * TK] - TK <= qpos[2 * TK]}
    slot_head = lax.broadcasted_iota(jnp.int32, (LANES, TQ), 0) // BIAS_SLOTS
    qbt = qbt_ref[0, 0]
    for h in range(FOX_HEADS):
        qb_scr[h] = jnp.where(slot_head == h, qbt, jnp.zeros_like(qbt))
    m_scr[...] = jnp.full(m_scr.shape, NEG_INF, F32)
    acc_scr[...] = jnp.zeros_like(acc_scr)
    ones_rows = {nk: jnp.ones((BF16_ROWS, nk), BF16) for nk in (TK, 2 * TK)}

    n_keys = {"pair": 2 * TK, "single": TK}
    pair = lambda u: (u, "pair")

    def q_op(unit, par, h):
        idx, kind = unit
        nk = n_keys[kind]
        c0 = h * FOX_HEAD_DIM
        k0 = pl.multiple_of(idx * nk, nk)
        ka = jnp.concatenate([k_ref[0, pl.ds(k0, nk), c0:c0 + FOX_HEAD_DIM],
                              kb_ref[0, pl.ds(k0, nk), :]], axis=1)
        qa = jnp.concatenate([qt_ref[0, 0, c0:c0 + FOX_HEAD_DIM, :], qb_scr[h]], axis=0)
        st = _dot(ka, qa)
        s_scr[par, h, :nk, :] = st
        ml_scr[par, h, 0:1, :] = jnp.max(st, axis=0, keepdims=True)

    def xv_op(unit, par, h, masked):
        idx, kind = unit
        nk = n_keys[kind]
        c0 = h * FOX_HEAD_DIM
        st = s_scr[par, h, :nk, :]
        if masked:
            st = jnp.where(diag_mask[kind], st, NEG_INF)
            m_loc = jnp.max(st, axis=0, keepdims=True)
        else:
            m_loc = ml_scr[par, h, 0:1, :]
        m_old = m_scr[h, 0:1, :]
        m_new = jnp.maximum(m_old, m_loc)
        alpha = jnp.exp2(m_old - m_new)
        pt = jnp.exp2(st - m_new).astype(BF16)
        m_scr[h, 0:1, :] = m_new
        chunk = idx if kind == "pair" else idx // 2
        va = jnp.concatenate([vt_ref[0, chunk, c0:c0 + FOX_HEAD_DIM, :nk], ones_rows[nk]],
                             axis=0)
        acc_scr[h] = alpha * acc_scr[h] + _dot(va, pt)

    def slot(q=None, xv=None):
        for h in range(FOX_HEADS):
            if q is not None:
                q_op(q[0], q[1], h)
        for h in range(FOX_HEADS):
            if xv is not None:
                xv_op(xv[0], xv[1], h, xv[2])

    n_full = qi // 2

    @pl.when(n_full >= 1)
    def _():
        slot(q=(pair(0), 0))

    @pl.loop(0, jnp.maximum(n_full - 1, 0) // 2)
    def _(i):
        t = 2 * i
        slot(q=(pair(t + 1), 1), xv=(pair(t), 0, False))
        slot(q=(pair(t + 2), 0), xv=(pair(t + 1), 1, False))

    odd_blocks = qi % 2 == 0
    for last, is_kind in (((qi, "single"), odd_blocks),
                          ((n_full, "pair"), jnp.logical_not(odd_blocks))):
        @pl.when(is_kind & (n_full == 0))
        def _():
            slot(q=(last, 0))
            slot(xv=(last, 0, True))

        @pl.when(is_kind & (n_full % 2 == 1))
        def _():
            slot(q=(last, 1), xv=(pair(n_full - 1), 0, False))
            slot(xv=(last, 1, True))

        @pl.when(is_kind & (n_full >= 2) & (n_full % 2 == 0))
        def _():
            slot(q=(pair(n_full - 1), 1), xv=(pair(n_full - 2), 0, False))
            slot(q=(last, 0), xv=(pair(n_full - 1), 1, False))
            slot(xv=(last, 0, True))

    for h in range(FOX_HEADS):
        c0 = h * FOX_HEAD_DIM
        acc = acc_scr[h]
        ot = acc[:FOX_HEAD_DIM] / acc[FOX_HEAD_DIM:FOX_HEAD_DIM + 1]
        o = ot.T * zb_ref[0, :, c0:c0 + FOX_HEAD_DIM].astype(F32)
        o_scr[:, c0:c0 + FOX_HEAD_DIM] = o.astype(BF16)

    pb = _dot(o_scr[...], wpb_ref[...])
    merged = gb_ref[0].astype(F32) * pb + ma_ref[0].astype(F32)
    hres = x_ref[0] + _dot(merged.astype(BF16), wout_ref[...])
    ms = jnp.mean(hres * hres, axis=-1, keepdims=True)
    out_ref[0] = hres * lax.rsqrt(ms + EPS) * gf_ref[...]


def _resident(shape):
    return pl.BlockSpec(shape, lambda b, s: (0,) * len(shape), pipeline_mode=pl.Buffered(1))


def _bias_slot_constants():
    pk = np.zeros((LANES, LANES), np.float32)
    pq = np.zeros((LANES, LANES), np.float32)
    kones = np.zeros((1, LANES), np.float32)
    qones = np.zeros((LANES, TS), np.float32)
    for h in range(FOX_HEADS):
        for i in range(3):
            pk[FOX_HEADS * i + h, BIAS_SLOTS * h + i] = -1.0
            qones[BIAS_SLOTS * h + i, :] = 1.0
            pq[BIAS_SLOTS * h + 3 + i, FOX_HEADS * i + h] = 1.0
            kones[0, BIAS_SLOTS * h + 3 + i] = 1.0
    return (jnp.asarray(pk, BF16), jnp.asarray(kones), jnp.asarray(pq, BF16), jnp.asarray(qones))


def kernel(x, norm1_g, w_in, sgu_ln_g, sgu_ln_b, w_spatial, b_spatial, b_forget, b_gate,
           w_proj_a, w_proj_b, w_out, norm_f_g):
    B, S, D = x.shape
    assert D == D_MODEL and S % TS == 0 and TQ == TK and KEY_BLOCKS_PER_CHUNK == 2
    H = FOX_HEADS

    w_bf = w_in.astype(BF16)
    w_cols = lambda width, idx: pl.BlockSpec((D, width), lambda b, s: (0, idx),
                                             pipeline_mode=pl.Buffered(1))
    w_f = jnp.pad(w_bf[:, 7 * D:7 * D + H], ((0, 0), (0, BF16_ROWS - H)))
    w_qvf = jnp.concatenate([w_bf[:, 3 * D:4 * D], w_bf[:, 5 * D:6 * D], w_f], axis=1)
    w_g = w_bf[:, 7 * D + H:]
    b_ft = jnp.broadcast_to(b_forget[:, None], (H, TS))
    row = lambda t: t.reshape(1, -1)
    pk, kones, pq, qones = _bias_slot_constants()

    NS = S // TS
    tok = lambda: pl.BlockSpec((1, TS, D), lambda b, s: (b, s, 0))
    feat = lambda rows: pl.BlockSpec((1, 1, rows, TS), lambda b, s: (b, s, 0, 0))
    tok_out = jax.ShapeDtypeStruct((B, S, D), BF16)
    feat_out = lambda rows: jax.ShapeDtypeStruct((B, NS, rows, TS), BF16)
    qt, qbt, k, kb, vt, zb, gb, ma = pl.pallas_call(
        _proj_kernel,
        grid=(B, NS),
        in_specs=[
            tok(),
            _resident((1, D)), w_cols(3 * D, 0), w_cols(D, 4), w_cols(D, 6),
            _resident((D, 2 * D + BF16_ROWS)), _resident((D, 2 * D)),
            _resident((1, D)), _resident((1, D)),
            _resident((A_GROUPS, SGU_BLOCK, SGU_BLOCK)), _resident((SGU_BLOCK, A_GROUPS)),
            _resident((H, TS)), _resident((1, 2 * D)), _resident((D, D)),
            _resident((LANES, LANES)), _resident((1, LANES)), _resident((LANES, LANES)),
            _resident((LANES, TS)),
        ],
        out_specs=[feat(D), feat(LANES), tok(),
                   pl.BlockSpec((1, TS, LANES), lambda b, s: (b, s, 0)),
                   feat(D), tok(), tok(), tok()],
        out_shape=[feat_out(D), feat_out(LANES), tok_out,
                   jax.ShapeDtypeStruct((B, S, LANES), BF16),
                   feat_out(D), tok_out, tok_out, tok_out],
        scratch_shapes=[pltpu.VMEM((TS, D), BF16), pltpu.VMEM((H, TS), F32)],
        compiler_params=pltpu.CompilerParams(
            dimension_semantics=("arbitrary", "arbitrary"), vmem_limit_bytes=VMEM_LIMIT_BYTES),
        name="proj_sgu",
    )(x, row(norm1_g), w_bf, w_bf, w_bf, w_qvf, w_g,
      row(sgu_ln_g), row(sgu_ln_b), w_spatial,
      b_spatial.T, b_ft, b_gate.reshape(1, 2 * D), w_proj_a.astype(BF16), pk, kones, pq, qones)

    qtok = lambda: pl.BlockSpec((1, TQ, D), lambda b, s: (b, s, 0))
    qfeat = lambda rows: pl.BlockSpec(
        (1, 1, rows, TQ),
        lambda b, s: (b, s // KEY_BLOCKS_PER_CHUNK, 0, s % KEY_BLOCKS_PER_CHUNK))
    per_batch = lambda shape, **kw: pl.BlockSpec((1,) + shape,
                                                 lambda b, s: (b,) + (0,) * len(shape), **kw)
    return pl.pallas_call(
        _attn_kernel,
        grid=(B, S // TQ),
        in_specs=[
            qfeat(D), qfeat(LANES),
            per_batch((S, D)), per_batch((S, LANES), pipeline_mode=pl.Buffered(1)),
            per_batch((NS, D, TS), pipeline_mode=pl.Buffered(1)),
            qtok(), qtok(), qtok(), qtok(),
            _resident((D, D)), _resident((D, D)), _resident((1, D)),
        ],
        out_specs=qtok(),
        out_shape=jax.ShapeDtypeStruct((B, S, D), x.dtype),
        scratch_shapes=[pltpu.VMEM((TQ, D), BF16),
                        pltpu.VMEM((H, LANES, TQ), BF16),
                        pltpu.VMEM((H, 8, TQ), F32),
                        pltpu.VMEM((H, ACC_ROWS, TQ), F32),
                        pltpu.VMEM((2, H, 2 * TK, TQ), F32), pltpu.VMEM((2, H, 8, TQ), F32)],
        compiler_params=pltpu.CompilerParams(
            dimension_semantics=("arbitrary", "arbitrary"), vmem_limit_bytes=VMEM_LIMIT_BYTES),
        name="fox_merge",
    )(qt, qbt, k, kb, vt, zb, gb, ma, x, w_proj_b.astype(BF16), w_out.astype(BF16),
      row(norm_f_g))
```

```python
import math

import numpy as np
import jax
import jax.numpy as jnp
from jax import lax
from jax.experimental import pallas as pl
from jax.experimental.pallas import tpu as pltpu

D_MODEL = 1024
CHUNK = 64
SGU_BLOCK = 128
A_GROUPS = 8
A_GROUP_DIM = D_MODEL // A_GROUPS
FOX_HEADS = 8
FOX_HEAD_DIM = D_MODEL // FOX_HEADS
EPS = 1e-6
NEG_INF = -1e30
LOG2E = math.log2(math.e)

LANES = 128
BF16_ROWS = 16
TS = 512
TQ = 256
TK = 256
KEY_BLOCKS_PER_CHUNK = TS // TK
VMEM_LIMIT_BYTES = 60000 * 1024
BIAS_SLOTS = LANES // FOX_HEADS
ACC_ROWS = FOX_HEAD_DIM + BF16_ROWS

F32 = jnp.float32
BF16 = jnp.bfloat16


def _dot(a, b):
    return jnp.dot(a, b, preferred_element_type=F32)


def _dot_tn(w, t):
    return lax.dot_general(w, t, (((0,), (1,)), ((), ())), preferred_element_type=F32)


def _dot_tt(a, b):
    return lax.dot_general(a, b, (((0,), (0,)), ((), ())), preferred_element_type=F32)


def _sigmoid(t):
    return 1.0 / (1.0 + jnp.exp(-t))


def _split3_rows(t):
    hi = t.astype(BF16).astype(F32)
    r1 = t - hi
    mid = r1.astype(BF16).astype(F32)
    lo = (r1 - mid).astype(BF16).astype(F32)
    pad = jnp.zeros((LANES - 3 * t.shape[0], t.shape[1]), F32)
    return jnp.concatenate([hi, mid, lo, pad], axis=0).astype(BF16)


def _proj_kernel(x_ref, g1_ref, wsgu_ref, wk_ref, wzb_ref, wqvf_ref, wg_ref,
                 lng_ref, lnb_ref,
                 ws_ref, bst_ref, bft_ref, bg_ref, wpa_ref, pk_ref, kones_ref, pq_ref, qones_ref,
                 qt_ref, qbt_ref, k_ref, kb_ref, vt_ref, zb_ref, gb_ref, ma_ref,
                 a_scr, carry_scr):
    @pl.when(pl.program_id(1) == 0)
    def _():
        carry_scr[...] = jnp.zeros_like(carry_scr)

    x = x_ref[0]
    ms = jnp.mean(x * x, axis=-1, keepdims=True)
    xn = (x * lax.rsqrt(ms + EPS) * g1_ref[...]).astype(BF16)

    uvz = _dot(xn, wsgu_ref[...])
    pos = lax.broadcasted_iota(jnp.int32, (SGU_BLOCK, SGU_BLOCK), 0) // CHUNK
    src = lax.broadcasted_iota(jnp.int32, (SGU_BLOCK, SGU_BLOCK), 1) // CHUNK
    causal = pos >= src
    for g in range(A_GROUPS):
        c0 = g * A_GROUP_DIM
        ws = jnp.where(causal, ws_ref[g], 0.0).astype(BF16)
        bias = bst_ref[:, g:g + 1]
        lng = lng_ref[:, c0:c0 + A_GROUP_DIM]
        lnb = lnb_ref[:, c0:c0 + A_GROUP_DIM]
        vns = []
        for n in range(TS // SGU_BLOCK):
            r0 = n * SGU_BLOCK
            v = uvz[r0:r0 + SGU_BLOCK, D_MODEL + c0:D_MODEL + c0 + A_GROUP_DIM]
            mu = jnp.mean(v, axis=-1, keepdims=True)
            d = v - mu
            var = jnp.mean(d * d, axis=-1, keepdims=True)
            vns.append(((d * lax.rsqrt(var + EPS)) * lng + lnb).astype(BF16))
        y_all = _dot(ws, jnp.concatenate(vns, axis=1))
        for n in range(TS // SGU_BLOCK):
            r0 = n * SGU_BLOCK
            u = uvz[r0:r0 + SGU_BLOCK, c0:c0 + A_GROUP_DIM]
            z = uvz[r0:r0 + SGU_BLOCK, 2 * D_MODEL + c0:2 * D_MODEL + c0 + A_GROUP_DIM]
            y = y_all[:, n * A_GROUP_DIM:(n + 1) * A_GROUP_DIM] + bias
            a = u * y * (z * _sigmoid(z))
            a_scr[r0:r0 + SGU_BLOCK, c0:c0 + A_GROUP_DIM] = a.astype(BF16)

    gates = _dot(xn, wg_ref[...]) + bg_ref[...]
    ma = _sigmoid(gates[:, :D_MODEL]) * _dot(a_scr[...], wpa_ref[...])
    ma_ref[0] = ma.astype(BF16)
    gb_ref[0] = _sigmoid(gates[:, D_MODEL:]).astype(BF16)

    k_ref[0] = _dot(xn, wk_ref[...]).astype(BF16)
    zb = _dot(xn, wzb_ref[...])
    zb_ref[0] = (zb * _sigmoid(zb)).astype(BF16)
    qvft = _dot_tn(wqvf_ref[...], xn)
    qt_ref[0, 0] = (qvft[:D_MODEL] * (LOG2E / math.sqrt(FOX_HEAD_DIM))).astype(BF16)
    vt_ref[0, 0] = qvft[D_MODEL:2 * D_MODEL].astype(BF16)

    f = qvft[2 * D_MODEL:2 * D_MODEL + FOX_HEADS] + bft_ref[...]
    logf = jnp.minimum(f, 0.0) - jnp.log1p(jnp.exp(-jnp.abs(f)))
    triu = (lax.broadcasted_iota(jnp.int32, (TS, TS), 0)
            <= lax.broadcasted_iota(jnp.int32, (TS, TS), 1)).astype(BF16)
    r = _dot(_split3_rows(logf), triu)
    c = carry_scr[...] + (r[0:FOX_HEADS] + r[FOX_HEADS:2 * FOX_HEADS]
                          + r[2 * FOX_HEADS:3 * FOX_HEADS])
    carry_scr[...] = jnp.broadcast_to(c[:, TS - 1:TS], c.shape)
    parts = _split3_rows(c * LOG2E)
    qb = qones_ref[...] + _dot(pq_ref[...], parts)
    qbt_ref[0, 0] = qb.astype(BF16)
    kb = kones_ref[...] + _dot_tt(parts, pk_ref[...])
    kb_ref[0] = kb.astype(BF16)


def _attn_kernel(qt_ref, qbt_ref, k_ref, kb_ref, vt_lo_ref, vt_hi_ref, zb_ref, gb_ref, ma_ref,
                 x_ref,
                 wpb_ref, wout_ref, gf_ref, out_ref,
                 o_scr, qb_scr, m_scr, acc_scr, s_scr, ml_scr):
    qi = pl.program_id(1)
    kpos = {nk: lax.broadcasted_iota(jnp.int32, (nk, TQ), 0) for nk in (TK, 2 * TK)}
    qpos = {nk: lax.broadcasted_iota(jnp.int32, (nk, TQ), 1) for nk in (TK, 2 * TK)}
    diag_mask = {"single": kpos[TK] <= qpos[TK], "pair": kpos[2 * TK] - TK <= qpos[2 * TK]}
    slot_head = lax.broadcasted_iota(jnp.int32, (LANES, TQ), 0) // BIAS_SLOTS
    qbt = qbt_ref[0, 0]
    for h in range(FOX_HEADS):
        qb_scr[h] = jnp.where(slot_head == h, qbt, jnp.zeros_like(qbt))
    m_scr[...] = jnp.full(m_scr.shape, NEG_INF, F32)
    acc_scr[...] = jnp.zeros_like(acc_scr)
    ones_rows = {nk: jnp.ones((BF16_ROWS, nk), BF16) for nk in (TK, 2 * TK)}

    n_keys = {"pair": 2 * TK, "single": TK}
    pair = lambda u: (u, "pair")

    def q_op(unit, par, h):
        idx, kind = unit
        nk = n_keys[kind]
        c0 = h * FOX_HEAD_DIM
        k0 = pl.multiple_of(idx * nk, nk)
        ka = jnp.concatenate([k_ref[0, pl.ds(k0, nk), c0:c0 + FOX_HEAD_DIM],
                              kb_ref[0, pl.ds(k0, nk), :]], axis=1)
        qa = jnp.concatenate([qt_ref[0, 0, c0:c0 + FOX_HEAD_DIM, :], qb_scr[h]], axis=0)
        st = _dot(ka, qa)
        s_scr[par, h, :nk, :] = st
        ml_scr[par, h, 0:1, :] = jnp.max(st, axis=0, keepdims=True)

    def xv_op(unit, par, h, masked):
        idx, kind = unit
        nk = n_keys[kind]
        c0 = h * FOX_HEAD_DIM
        st = s_scr[par, h, :nk, :]
        if masked:
            st = jnp.where(diag_mask[kind], st, NEG_INF)
            m_loc = jnp.max(st, axis=0, keepdims=True)
        else:
            m_loc = ml_scr[par, h, 0:1, :]
        m_old = m_scr[h, 0:1, :]
        m_new = jnp.maximum(m_old, m_loc)
        alpha = jnp.exp2(m_old - m_new)
        pt = jnp.exp2(st - m_new).astype(BF16)
        m_scr[h, 0:1, :] = m_new
        chunk = idx if kind == "pair" else idx // 2
        half_rows = D_MODEL // 2
        vt_half, r0 = (vt_lo_ref, c0) if c0 < half_rows else (vt_hi_ref, c0 - half_rows)
        va = jnp.concatenate([vt_half[0, chunk, r0:r0 + FOX_HEAD_DIM, :nk], ones_rows[nk]],
                             axis=0)
        acc_scr[h] = alpha * acc_scr[h] + _dot(va, pt)

    def slot(q=None, xv=None):
        for h in range(FOX_HEADS):
            if q is not None:
                q_op(q[0], q[1], h)
        for h in range(FOX_HEADS):
            if xv is not None:
                xv_op(xv[0], xv[1], h, xv[2])

    n_full = qi // 2

    @pl.when(n_full >= 1)
    def _():
        slot(q=(pair(0), 0))

    @pl.loop(0, jnp.maximum(n_full - 1, 0) // 2)
    def _(i):
        t = 2 * i
        slot(q=(pair(t + 1), 1), xv=(pair(t), 0, False))
        slot(q=(pair(t + 2), 0), xv=(pair(t + 1), 1, False))

    odd_blocks = qi % 2 == 0
    for last, is_kind in (((qi, "single"), odd_blocks),
                          ((n_full, "pair"), jnp.logical_not(odd_blocks))):
        @pl.when(is_kind & (n_full == 0))
        def _():
            slot(q=(last, 0))
            slot(xv=(last, 0, True))

        @pl.when(is_kind & (n_full % 2 == 1))
        def _():
            slot(q=(last, 1), xv=(pair(n_full - 1), 0, False))
            slot(xv=(last, 1, True))

        @pl.when(is_kind & (n_full >= 2) & (n_full % 2 == 0))
        def _():
            slot(q=(pair(n_full - 1), 1), xv=(pair(n_full - 2), 0, False))
            slot(q=(last, 0), xv=(pair(n_full - 1), 1, False))
            slot(xv=(last, 0, True))

    for h in range(FOX_HEADS):
        c0 = h * FOX_HEAD_DIM
        acc = acc_scr[h]
        ot = acc[:FOX_HEAD_DIM] / acc[FOX_HEAD_DIM:FOX_HEAD_DIM + 1]
        o = ot.T * zb_ref[0, :, c0:c0 + FOX_HEAD_DIM].astype(F32)
        o_scr[:, c0:c0 + FOX_HEAD_DIM] = o.astype(BF16)

    pb = _dot(o_scr[...], wpb_ref[...])
    merged = gb_ref[0].astype(F32) * pb + ma_ref[0].astype(F32)
    hres = x_ref[0] + _dot(merged.astype(BF16), wout_ref[...])
    ms = jnp.mean(hres * hres, axis=-1, keepdims=True)
    out_ref[0] = hres * lax.rsqrt(ms + EPS) * gf_ref[...]


def _resident(shape):
    return pl.BlockSpec(shape, lambda b, s: (0,) * len(shape), pipeline_mode=pl.Buffered(1))


def _bias_slot_constants():
    pk = np.zeros((LANES, LANES), np.float32)
    pq = np.zeros((LANES, LANES), np.float32)
    kones = np.zeros((1, LANES), np.float32)
    qones = np.zeros((LANES, TS), np.float32)
    for h in range(FOX_HEADS):
        for i in range(3):
            pk[FOX_HEADS * i + h, BIAS_SLOTS * h + i] = -1.0
            qones[BIAS_SLOTS * h + i, :] = 1.0
            pq[BIAS_SLOTS * h + 3 + i, FOX_HEADS * i + h] = 1.0
            kones[0, BIAS_SLOTS * h + 3 + i] = 1.0
    return (jnp.asarray(pk, BF16), jnp.asarray(kones), jnp.asarray(pq, BF16), jnp.asarray(qones))


def kernel(x, norm1_g, w_in, sgu_ln_g, sgu_ln_b, w_spatial, b_spatial, b_forget, b_gate,
           w_proj_a, w_proj_b, w_out, norm_f_g):
    B, S, D = x.shape
    assert D == D_MODEL and S % TS == 0 and TQ == TK and KEY_BLOCKS_PER_CHUNK == 2
    H = FOX_HEADS

    w_bf = w_in.astype(BF16)
    w_cols = lambda width, idx: pl.BlockSpec((D, width), lambda b, s: (0, idx),
                                             pipeline_mode=pl.Buffered(1))
    w_f = jnp.pad(w_bf[:, 7 * D:7 * D + H], ((0, 0), (0, BF16_ROWS - H)))
    w_qvf = jnp.concatenate([w_bf[:, 3 * D:4 * D], w_bf[:, 5 * D:6 * D], w_f], axis=1)
    w_g = w_bf[:, 7 * D + H:]
    b_ft = jnp.broadcast_to(b_forget[:, None], (H, TS))
    row = lambda t: t.reshape(1, -1)
    pk, kones, pq, qones = _bias_slot_constants()

    NS = S // TS
    tok = lambda: pl.BlockSpec((1, TS, D), lambda b, s: (b, s, 0))
    feat = lambda rows: pl.BlockSpec((1, 1, rows, TS), lambda b, s: (b, s, 0, 0))
    tok_out = jax.ShapeDtypeStruct((B, S, D), BF16)
    feat_out = lambda rows: jax.ShapeDtypeStruct((B, NS, rows, TS), BF16)
    qt, qbt, k, kb, vt, zb, gb, ma = pl.pallas_call(
        _proj_kernel,
        grid=(B, NS),
        in_specs=[
            tok(),
            _resident((1, D)), w_cols(3 * D, 0), w_cols(D, 4), w_cols(D, 6),
            _resident((D, 2 * D + BF16_ROWS)), _resident((D, 2 * D)),
            _resident((1, D)), _resident((1, D)),
            _resident((A_GROUPS, SGU_BLOCK, SGU_BLOCK)), _resident((SGU_BLOCK, A_GROUPS)),
            _resident((H, TS)), _resident((1, 2 * D)), _resident((D, D)),
            _resident((LANES, LANES)), _resident((1, LANES)), _resident((LANES, LANES)),
            _resident((LANES, TS)),
        ],
        out_specs=[feat(D), feat(LANES), tok(),
                   pl.BlockSpec((1, TS, LANES), lambda b, s: (b, s, 0)),
                   feat(D), tok(), tok(), tok()],
        out_shape=[feat_out(D), feat_out(LANES), tok_out,
                   jax.ShapeDtypeStruct((B, S, LANES), BF16),
                   feat_out(D), tok_out, tok_out, tok_out],
        scratch_shapes=[pltpu.VMEM((TS, D), BF16), pltpu.VMEM((H, TS), F32)],
        compiler_params=pltpu.CompilerParams(
            dimension_semantics=("arbitrary", "arbitrary"), vmem_limit_bytes=VMEM_LIMIT_BYTES),
        name="proj_sgu",
    )(x, row(norm1_g), w_bf, w_bf, w_bf, w_qvf, w_g,
      row(sgu_ln_g), row(sgu_ln_b), w_spatial,
      b_spatial.T, b_ft, b_gate.reshape(1, 2 * D), w_proj_a.astype(BF16), pk, kones, pq, qones)

    qtok = lambda: pl.BlockSpec((1, TQ, D), lambda b, s: (b, s, 0))
    qfeat = lambda rows: pl.BlockSpec(
        (1, 1, rows, TQ),
        lambda b, s: (b, s // KEY_BLOCKS_PER_CHUNK, 0, s % KEY_BLOCKS_PER_CHUNK))
    per_batch = lambda shape, **kw: pl.BlockSpec((1,) + shape,
                                                 lambda b, s: (b,) + (0,) * len(shape), **kw)
    vt_half = lambda half, **kw: pl.BlockSpec((1, NS, D // 2, TS), lambda b, s: (b, 0, half, 0),
                                              **kw)
    return pl.pallas_call(
        _attn_kernel,
        grid=(B, S // TQ),
        in_specs=[
            qfeat(D), qfeat(LANES),
            per_batch((S, D)), per_batch((S, LANES), pipeline_mode=pl.Buffered(1)),
            vt_half(0), vt_half(1, pipeline_mode=pl.Buffered(1)),
            qtok(), qtok(), qtok(), qtok(),
            _resident((D, D)), _resident((D, D)), _resident((1, D)),
        ],
        out_specs=qtok(),
        out_shape=jax.ShapeDtypeStruct((B, S, D), x.dtype),
        scratch_shapes=[pltpu.VMEM((TQ, D), BF16),
                        pltpu.VMEM((H, LANES, TQ), BF16),
                        pltpu.VMEM((H, 8, TQ), F32),
                        pltpu.VMEM((H, ACC_ROWS, TQ), F32),
                        pltpu.VMEM((2, H, 2 * TK, TQ), F32), pltpu.VMEM((2, H, 8, TQ), F32)],
        compiler_params=pltpu.CompilerParams(
            dimension_semantics=("arbitrary", "arbitrary"), vmem_limit_bytes=VMEM_LIMIT_BYTES),
        name="fox_merge",
    )(qt, qbt, k, kb, vt, vt, zb, gb, ma, x, w_proj_b.astype(BF16), w_out.astype(BF16),
      row(norm_f_g))
```

```python
import math

import numpy as np
import jax
import jax.numpy as jnp
from jax import lax
from jax.experimental import pallas as pl
from jax.experimental.pallas import tpu as pltpu

D_MODEL = 1024
CHUNK = 64
SGU_BLOCK = 128
A_GROUPS = 8
A_GROUP_DIM = D_MODEL // A_GROUPS
FOX_HEADS = 8
FOX_HEAD_DIM = D_MODEL // FOX_HEADS
EPS = 1e-6
NEG_INF = -1e30
LOG2E = math.log2(math.e)

LANES = 128
BF16_ROWS = 16
TS = 512
TQ = 256
TK = 256
KEY_BLOCKS_PER_CHUNK = TS // TK
VMEM_LIMIT_BYTES = 60000 * 1024
BIAS_SLOTS = LANES // FOX_HEADS
ACC_ROWS = FOX_HEAD_DIM + BF16_ROWS

F32 = jnp.float32
BF16 = jnp.bfloat16


def _dot(a, b):
    return jnp.dot(a, b, preferred_element_type=F32)


def _dot_tn(w, t):
    return lax.dot_general(w, t, (((0,), (1,)), ((), ())), preferred_element_type=F32)


def _dot_tt(a, b):
    return lax.dot_general(a, b, (((0,), (0,)), ((), ())), preferred_element_type=F32)


def _sigmoid(t):
    return 1.0 / (1.0 + jnp.exp(-t))


def _split3_rows(t):
    hi = t.astype(BF16).astype(F32)
    r1 = t - hi
    mid = r1.astype(BF16).astype(F32)
    lo = (r1 - mid).astype(BF16).astype(F32)
    pad = jnp.zeros((LANES - 3 * t.shape[0], t.shape[1]), F32)
    return jnp.concatenate([hi, mid, lo, pad], axis=0).astype(BF16)


def _proj_kernel(x_ref, g1_ref, wsgu_ref, wk_ref, wzb_ref, wqvf_ref, wg_ref,
                 lng_ref, lnb_ref,
                 ws_ref, bst_ref, bft_ref, bg_ref, wpa_ref, pk_ref, kones_ref, pq_ref, qones_ref,
                 qt_ref, qbt_ref, k_ref, kb_ref, vt_ref, zb_ref, gb_ref, ma_ref,
                 a_scr, carry_scr):
    @pl.when(pl.program_id(1) == 0)
    def _():
        carry_scr[...] = jnp.zeros_like(carry_scr)

    x = x_ref[0]
    ms = jnp.mean(x * x, axis=-1, keepdims=True)
    xn = (x * lax.rsqrt(ms + EPS) * g1_ref[...]).astype(BF16)

    uvz = _dot(xn, wsgu_ref[...])
    pos = lax.broadcasted_iota(jnp.int32, (SGU_BLOCK, SGU_BLOCK), 0) // CHUNK
    src = lax.broadcasted_iota(jnp.int32, (SGU_BLOCK, SGU_BLOCK), 1) // CHUNK
    causal = pos >= src
    for g in range(A_GROUPS):
        c0 = g * A_GROUP_DIM
        ws = jnp.where(causal, ws_ref[g], 0.0).astype(BF16)
        bias = bst_ref[:, g:g + 1]
        lng = lng_ref[:, c0:c0 + A_GROUP_DIM]
        lnb = lnb_ref[:, c0:c0 + A_GROUP_DIM]
        vns = []
        for n in range(TS // SGU_BLOCK):
            r0 = n * SGU_BLOCK
            v = uvz[r0:r0 + SGU_BLOCK, D_MODEL + c0:D_MODEL + c0 + A_GROUP_DIM]
            mu = jnp.mean(v, axis=-1, keepdims=True)
            d = v - mu
            var = jnp.mean(d * d, axis=-1, keepdims=True)
            vns.append(((d * lax.rsqrt(var + EPS)) * lng + lnb).astype(BF16))
        y_all = _dot(ws, jnp.concatenate(vns, axis=1))
        for n in range(TS // SGU_BLOCK):
            r0 = n * SGU_BLOCK
            u = uvz[r0:r0 + SGU_BLOCK, c0:c0 + A_GROUP_DIM]
            z = uvz[r0:r0 + SGU_BLOCK, 2 * D_MODEL + c0:2 * D_MODEL + c0 + A_GROUP_DIM]
            y = y_all[:, n * A_GROUP_DIM:(n + 1) * A_GROUP_DIM] + bias
            a = u * y * (z * _sigmoid(z))
            a_scr[r0:r0 + SGU_BLOCK, c0:c0 + A_GROUP_DIM] = a.astype(BF16)

    gates = _dot(xn, wg_ref[...]) + bg_ref[...]
    ma = _sigmoid(gates[:, :D_MODEL]) * _dot(a_scr[...], wpa_ref[...])
    ma_ref[0] = ma.astype(BF16)
    gb_ref[0] = _sigmoid(gates[:, D_MODEL:]).astype(BF16)

    k_ref[0] = _dot(xn, wk_ref[...]).astype(BF16)
    zb = _dot(xn, wzb_ref[...])
    zb_ref[0] = (zb * _sigmoid(zb)).astype(BF16)
    qvft = _dot_tn(wqvf_ref[...], xn)
    qt_ref[0, 0] = (qvft[:D_MODEL] * (LOG2E / math.sqrt(FOX_HEAD_DIM))).astype(BF16)
    vt_ref[0, 0] = qvft[D_MODEL:2 * D_MODEL].astype(BF16)

    f = qvft[2 * D_MODEL:2 * D_MODEL + FOX_HEADS] + bft_ref[...]
    logf = jnp.minimum(f, 0.0) - jnp.log1p(jnp.exp(-jnp.abs(f)))
    triu = (lax.broadcasted_iota(jnp.int32, (TS, TS), 0)
            <= lax.broadcasted_iota(jnp.int32, (TS, TS), 1)).astype(BF16)
    r = _dot(_split3_rows(logf), triu)
    c = carry_scr[...] + (r[0:FOX_HEADS] + r[FOX_HEADS:2 * FOX_HEADS]
                          + r[2 * FOX_HEADS:3 * FOX_HEADS])
    carry_scr[...] = jnp.broadcast_to(c[:, TS - 1:TS], c.shape)
    parts = _split3_rows(c * LOG2E)
    qb = qones_ref[...] + _dot(pq_ref[...], parts)
    qbt_ref[0, 0] = qb.astype(BF16)
    kb = kones_ref[...] + _dot_tt(parts, pk_ref[...])
    kb_ref[0] = kb.astype(BF16)


def _attn_kernel(qt_ref, qbt_ref, k_ref, kb_ref, vt_lo_ref, vt_hi_ref, zb_ref, gb_ref, ma_ref,
                 x_ref,
                 wpb_ref, wout_ref, gf_ref, out_ref,
                 o_scr, qb_scr, m_scr, acc_scr, s_scr, ml_scr):
    qi = pl.program_id(1)
    kpos = {nk: lax.broadcasted_iota(jnp.int32, (nk, TQ), 0) for nk in (TK, 2 * TK)}
    qpos = {nk: lax.broadcasted_iota(jnp.int32, (nk, TQ), 1) for nk in (TK, 2 * TK)}
    diag_mask = {"single": kpos[TK] <= qpos[TK], "pair": kpos[2 * TK] - TK <= qpos[2 * TK]}
    slot_head = lax.broadcasted_iota(jnp.int32, (LANES, TQ), 0) // BIAS_SLOTS
    qbt = qbt_ref[0, 0]
    for h in range(FOX_HEADS):
        qb_scr[h] = jnp.where(slot_head == h, qbt, jnp.zeros_like(qbt))
    m_scr[...] = jnp.full(m_scr.shape, NEG_INF, F32)
    acc_scr[...] = jnp.zeros_like(acc_scr)
    ones_rows = {nk: jnp.ones((BF16_ROWS, nk), BF16) for nk in (TK, 2 * TK)}

    n_keys = {"pair": 2 * TK, "single": TK}
    pair = lambda u: (u, "pair")

    def q_op(unit, par, h):
        idx, kind = unit
        nk = n_keys[kind]
        c0 = h * FOX_HEAD_DIM
        k0 = pl.multiple_of(idx * nk, nk)
        ka = jnp.concatenate([k_ref[0, pl.ds(k0, nk), c0:c0 + FOX_HEAD_DIM],
                              kb_ref[0, pl.ds(k0, nk), :]], axis=1)
        qa = jnp.concatenate([qt_ref[0, 0, c0:c0 + FOX_HEAD_DIM, :], qb_scr[h]], axis=0)
        st = _dot(ka, qa)
        s_scr[par, h, :nk, :] = st
        ml_scr[par, h, 0:1, :] = jnp.max(st, axis=0, keepdims=True)

    def xv_op(unit, par, h, masked):
        idx, kind = unit
        nk = n_keys[kind]
        c0 = h * FOX_HEAD_DIM
        st = s_scr[par, h, :nk, :]
        if masked:
            st = jnp.where(diag_mask[kind], st, NEG_INF)
            m_loc = jnp.max(st, axis=0, keepdims=True)
        else:
            m_loc = ml_scr[par, h, 0:1, :]
        m_old = m_scr[h, 0:1, :]
        m_new = jnp.maximum(m_old, m_loc)
        alpha = jnp.exp2(m_old - m_new)
        pt = jnp.exp2(st - m_new).astype(BF16)
        m_scr[h, 0:1, :] = m_new
        chunk = idx if kind == "pair" else idx // 2
        half_rows = D_MODEL // 2
        vt_half, r0 = (vt_lo_ref, c0) if c0 < half_rows else (vt_hi_ref, c0 - half_rows)
        va = jnp.concatenate([vt_half[0, chunk, r0:r0 + FOX_HEAD_DIM, :nk], ones_rows[nk]],
                             axis=0)
        acc_scr[h] = alpha * acc_scr[h] + _dot(va, pt)

    def slot(q=None, xv=None):
        for h in range(FOX_HEADS):
            if xv is not None:
                xv_op(xv[0], xv[1], h, xv[2])
            if q is not None:
                q_op(q[0], q[1], h)

    n_full = qi // 2

    @pl.when(n_full >= 1)
    def _():
        slot(q=(pair(0), 0))

    @pl.loop(0, jnp.maximum(n_full - 1, 0) // 2)
    def _(i):
        t = 2 * i
        slot(q=(pair(t + 1), 1), xv=(pair(t), 0, False))
        slot(q=(pair(t + 2), 0), xv=(pair(t + 1), 1, False))

    odd_blocks = qi % 2 == 0
    for last, is_kind in (((qi, "single"), odd_blocks),
                          ((n_full, "pair"), jnp.logical_not(odd_blocks))):
        @pl.when(is_kind & (n_full == 0))
        def _():
            slot(q=(last, 0))
            slot(xv=(last, 0, True))

        @pl.when(is_kind & (n_full % 2 == 1))
        def _():
            slot(q=(last, 1), xv=(pair(n_full - 1), 0, False))
            slot(xv=(last, 1, True))

        @pl.when(is_kind & (n_full >= 2) & (n_full % 2 == 0))
        def _():
            slot(q=(pair(n_full - 1), 1), xv=(pair(n_full - 2), 0, False))
            slot(q=(last, 0), xv=(pair(n_full - 1), 1, False))
            slot(xv=(last, 0, True))

    for h in range(FOX_HEADS):
        c0 = h * FOX_HEAD_DIM
        acc = acc_scr[h]
        ot = acc[:FOX_HEAD_DIM] / acc[FOX_HEAD_DIM:FOX_HEAD_DIM + 1]
        o = ot.T * zb_ref[0, :, c0:c0 + FOX_HEAD_DIM].astype(F32)
        o_scr[:, c0:c0 + FOX_HEAD_DIM] = o.astype(BF16)

    pb = _dot(o_scr[...], wpb_ref[...])
    merged = gb_ref[0].astype(F32) * pb + ma_ref[0].astype(F32)
    hres = x_ref[0] + _dot(merged.astype(BF16), wout_ref[...])
    ms = jnp.mean(hres * hres, axis=-1, keepdims=True)
    out_ref[0] = hres * lax.rsqrt(ms + EPS) * gf_ref[...]


def _resident(shape):
    return pl.BlockSpec(shape, lambda b, s: (0,) * len(shape), pipeline_mode=pl.Buffered(1))


def _bias_slot_constants():
    pk = np.zeros((LANES, LANES), np.float32)
    pq = np.zeros((LANES, LANES), np.float32)
    kones = np.zeros((1, LANES), np.float32)
    qones = np.zeros((LANES, TS), np.float32)
    for h in range(FOX_HEADS):
        for i in range(3):
            pk[FOX_HEADS * i + h, BIAS_SLOTS * h + i] = -1.0
            qones[BIAS_SLOTS * h + i, :] = 1.0
            pq[BIAS_SLOTS * h + 3 + i, FOX_HEADS * i + h] = 1.0
            kones[0, BIAS_SLOTS * h + 3 + i] = 1.0
    return (jnp.asarray(pk, BF16), jnp.asarray(kones), jnp.asarray(pq, BF16), jnp.asarray(qones))


def kernel(x, norm1_g, w_in, sgu_ln_g, sgu_ln_b, w_spatial, b_spatial, b_forget, b_gate,
           w_proj_a, w_proj_b, w_out, norm_f_g):
    B, S, D = x.shape
    assert D == D_MODEL and S % TS == 0 and TQ == TK and KEY_BLOCKS_PER_CHUNK == 2
    H = FOX_HEADS

    w_bf = w_in.astype(BF16)
    w_cols = lambda width, idx: pl.BlockSpec((D, width), lambda b, s: (0, idx),
                                             pipeline_mode=pl.Buffered(1))
    w_f = jnp.pad(w_bf[:, 7 * D:7 * D + H], ((0, 0), (0, BF16_ROWS - H)))
    w_qvf = jnp.concatenate([w_bf[:, 3 * D:4 * D], w_bf[:, 5 * D:6 * D], w_f], axis=1)
    w_g = w_bf[:, 7 * D + H:]
    b_ft = jnp.broadcast_to(b_forget[:, None], (H, TS))
    row = lambda t: t.reshape(1, -1)
    pk, kones, pq, qones = _bias_slot_constants()

    NS = S // TS
    tok = lambda: pl.BlockSpec((1, TS, D), lambda b, s: (b, s, 0))
    feat = lambda rows: pl.BlockSpec((1, 1, rows, TS), lambda b, s: (b, s, 0, 0))
    tok_out = jax.ShapeDtypeStruct((B, S, D), BF16)
    feat_out = lambda rows: jax.ShapeDtypeStruct((B, NS, rows, TS), BF16)
    qt, qbt, k, kb, vt, zb, gb, ma = pl.pallas_call(
        _proj_kernel,
        grid=(B, NS),
        in_specs=[
            tok(),
            _resident((1, D)), w_cols(3 * D, 0), w_cols(D, 4), w_cols(D, 6),
            _resident((D, 2 * D + BF16_ROWS)), _resident((D, 2 * D)),
            _resident((1, D)), _resident((1, D)),
            _resident((A_GROUPS, SGU_BLOCK, SGU_BLOCK)), _resident((SGU_BLOCK, A_GROUPS)),
            _resident((H, TS)), _resident((1, 2 * D)), _resident((D, D)),
            _resident((LANES, LANES)), _resident((1, LANES)), _resident((LANES, LANES)),
            _resident((LANES, TS)),
        ],
        out_specs=[feat(D), feat(LANES), tok(),
                   pl.BlockSpec((1, TS, LANES), lambda b, s: (b, s, 0)),
                   feat(D), tok(), tok(), tok()],
        out_shape=[feat_out(D), feat_out(LANES), tok_out,
                   jax.ShapeDtypeStruct((B, S, LANES), BF16),
                   feat_out(D), tok_out, tok_out, tok_out],
        scratch_shapes=[pltpu.VMEM((TS, D), BF16), pltpu.VMEM((H, TS), F32)],
        compiler_params=pltpu.CompilerParams(
            dimension_semantics=("arbitrary", "arbitrary"), vmem_limit_bytes=VMEM_LIMIT_BYTES),
        name="proj_sgu",
    )(x, row(norm1_g), w_bf, w_bf, w_bf, w_qvf, w_g,
      row(sgu_ln_g), row(sgu_ln_b), w_spatial,
      b_spatial.T, b_ft, b_gate.reshape(1, 2 * D), w_proj_a.astype(BF16), pk, kones, pq, qones)

    qtok = lambda: pl.BlockSpec((1, TQ, D), lambda b, s: (b, s, 0))
    qfeat = lambda rows: pl.BlockSpec(
        (1, 1, rows, TQ),
        lambda b, s: (b, s // KEY_BLOCKS_PER_CHUNK, 0, s % KEY_BLOCKS_PER_CHUNK))
    per_batch = lambda shape, **kw: pl.BlockSpec((1,) + shape,
                                                 lambda b, s: (b,) + (0,) * len(shape), **kw)
    vt_half = lambda half, **kw: pl.BlockSpec((1, NS, D // 2, TS), lambda b, s: (b, 0, half, 0),
                                              **kw)
    return pl.pallas_call(
        _attn_kernel,
        grid=(B, S // TQ),
        in_specs=[
            qfeat(D), qfeat(LANES),
            per_batch((S, D)), per_batch((S, LANES), pipeline_mode=pl.Buffered(1)),
            vt_half(0), vt_half(1, pipeline_mode=pl.Buffered(1)),
            qtok(), qtok(), qtok(), qtok(),
            _resident((D, D)), _resident((D, D)), _resident((1, D)),
        ],
        out_specs=qtok(),
        out_shape=jax.ShapeDtypeStruct((B, S, D), x.dtype),
        scratch_shapes=[pltpu.VMEM((TQ, D), BF16),
                        pltpu.VMEM((H, LANES, TQ), BF16),
                        pltpu.VMEM((H, 8, TQ), F32),
                        pltpu.VMEM((H, ACC_ROWS, TQ), F32),
                        pltpu.VMEM((2, H, 2 * TK, TQ), F32), pltpu.VMEM((2, H, 8, TQ), F32)],
        compiler_params=pltpu.CompilerParams(
            dimension_semantics=("arbitrary", "arbitrary"), vmem_limit_bytes=VMEM_LIMIT_BYTES),
        name="fox_merge",
    )(qt, qbt, k, kb, vt, vt, zb, gb, ma, x, w_proj_b.astype(BF16), w_out.astype(BF16),
      row(norm_f_g))
```

```python
import math

import numpy as np
import jax
import jax.numpy as jnp
from jax import lax
from jax.experimental import pallas as pl
from jax.experimental.pallas import tpu as pltpu

D_MODEL = 1024
CHUNK = 64
SGU_BLOCK = 128
A_GROUPS = 8
A_GROUP_DIM = D_MODEL // A_GROUPS
FOX_HEADS = 8
FOX_HEAD_DIM = D_MODEL // FOX_HEADS
EPS = 1e-6
NEG_INF = -1e30
LOG2E = math.log2(math.e)

LANES = 128
BF16_ROWS = 16
TS = 512
TQ = 256
TK = 256
KEY_BLOCKS_PER_CHUNK = TS // TK
VMEM_LIMIT_BYTES = 60000 * 1024
BIAS_SLOTS = LANES // FOX_HEADS
ACC_ROWS = FOX_HEAD_DIM + BF16_ROWS

F32 = jnp.float32
BF16 = jnp.bfloat16


def _dot(a, b):
    return jnp.dot(a, b, preferred_element_type=F32)


def _dot_tn(w, t):
    return lax.dot_general(w, t, (((0,), (1,)), ((), ())), preferred_element_type=F32)


def _dot_tt(a, b):
    return lax.dot_general(a, b, (((0,), (0,)), ((), ())), preferred_element_type=F32)


def _sigmoid(t):
    return 1.0 / (1.0 + jnp.exp(-t))


def _split3_rows(t):
    hi = t.astype(BF16).astype(F32)
    r1 = t - hi
    mid = r1.astype(BF16).astype(F32)
    lo = (r1 - mid).astype(BF16).astype(F32)
    pad = jnp.zeros((LANES - 3 * t.shape[0], t.shape[1]), F32)
    return jnp.concatenate([hi, mid, lo, pad], axis=0).astype(BF16)


def _proj_kernel(x_ref, g1_ref, wsgu_ref, wk_ref, wzb_ref, wqvf_ref, wg_ref,
                 lng_ref, lnb_ref,
                 ws_ref, bst_ref, bft_ref, bg_ref, wpa_ref, pk_ref, kones_ref, pq_ref, qones_ref,
                 qt_ref, qbt_ref, k_ref, kb_ref, vt_ref, zb_ref, gb_ref, ma_ref,
                 a_scr, carry_scr):
    @pl.when(pl.program_id(1) == 0)
    def _():
        carry_scr[...] = jnp.zeros_like(carry_scr)

    x = x_ref[0]
    ms = jnp.mean(x * x, axis=-1, keepdims=True)
    xn = (x * lax.rsqrt(ms + EPS) * g1_ref[...]).astype(BF16)

    uvz = _dot(xn, wsgu_ref[...])
    pos = lax.broadcasted_iota(jnp.int32, (SGU_BLOCK, SGU_BLOCK), 0) // CHUNK
    src = lax.broadcasted_iota(jnp.int32, (SGU_BLOCK, SGU_BLOCK), 1) // CHUNK
    causal = pos >= src
    for g in range(A_GROUPS):
        c0 = g * A_GROUP_DIM
        ws = jnp.where(causal, ws_ref[g], 0.0).astype(BF16)
        bias = bst_ref[:, g:g + 1]
        lng = lng_ref[:, c0:c0 + A_GROUP_DIM]
        lnb = lnb_ref[:, c0:c0 + A_GROUP_DIM]
        vns = []
        for n in range(TS // SGU_BLOCK):
            r0 = n * SGU_BLOCK
            v = uvz[r0:r0 + SGU_BLOCK, D_MODEL + c0:D_MODEL + c0 + A_GROUP_DIM]
            mu = jnp.mean(v, axis=-1, keepdims=True)
            d = v - mu
            var = jnp.mean(d * d, axis=-1, keepdims=True)
            vns.append(((d * lax.rsqrt(var + EPS)) * lng + lnb).astype(BF16))
        y_all = _dot(ws, jnp.concatenate(vns, axis=1))
        for n in range(TS // SGU_BLOCK):
            r0 = n * SGU_BLOCK
            u = uvz[r0:r0 + SGU_BLOCK, c0:c0 + A_GROUP_DIM]
            z = uvz[r0:r0 + SGU_BLOCK, 2 * D_MODEL + c0:2 * D_MODEL + c0 + A_GROUP_DIM]
            y = y_all[:, n * A_GROUP_DIM:(n + 1) * A_GROUP_DIM] + bias
            a = u * y * (z * _sigmoid(z))
            a_scr[r0:r0 + SGU_BLOCK, c0:c0 + A_GROUP_DIM] = a.astype(BF16)

    gates = _dot(xn, wg_ref[...]) + bg_ref[...]
    ma = _sigmoid(gates[:, :D_MODEL]) * _dot(a_scr[...], wpa_ref[...])
    ma_ref[0] = ma.astype(BF16)
    gb_ref[0] = _sigmoid(gates[:, D_MODEL:]).astype(BF16)

    k_ref[0] = _dot(xn, wk_ref[...]).astype(BF16)
    zb = _dot(xn, wzb_ref[...])
    zb_ref[0] = (zb * _sigmoid(zb)).astype(BF16)
    qvft = _dot_tn(wqvf_ref[...], xn)
    qt_ref[0, 0] = (qvft[:D_MODEL] * (LOG2E / math.sqrt(FOX_HEAD_DIM))).astype(BF16)
    vt_ref[0, 0] = qvft[D_MODEL:2 * D_MODEL].astype(BF16)

    f = qvft[2 * D_MODEL:2 * D_MODEL + FOX_HEADS] + bft_ref[...]
    logf = jnp.minimum(f, 0.0) - jnp.log1p(jnp.exp(-jnp.abs(f)))
    triu = (lax.broadcasted_iota(jnp.int32, (TS, TS), 0)
            <= lax.broadcasted_iota(jnp.int32, (TS, TS), 1)).astype(BF16)
    r = _dot(_split3_rows(logf), triu)
    c = carry_scr[...] + (r[0:FOX_HEADS] + r[FOX_HEADS:2 * FOX_HEADS]
                          + r[2 * FOX_HEADS:3 * FOX_HEADS])
    carry_scr[...] = jnp.broadcast_to(c[:, TS - 1:TS], c.shape)
    parts = _split3_rows(c * LOG2E)
    qb = qones_ref[...] + _dot(pq_ref[...], parts)
    qbt_ref[0, 0] = qb.astype(BF16)
    kb = kones_ref[...] + _dot_tt(parts, pk_ref[...])
    kb_ref[0] = kb.astype(BF16)


def _attn_kernel(qt_ref, qbt_ref, k_ref, kb_ref, vt_lo_ref, vt_hi_ref, zb_ref, gb_ref, ma_ref,
                 x_ref,
                 wpb_ref, wout_ref, gf_ref, out_ref,
                 o_scr, qb_scr, m_scr, acc_scr, s_scr, ml_scr):
    qi = pl.program_id(1)
    kpos = {nk: lax.broadcasted_iota(jnp.int32, (nk, TQ), 0) for nk in (TK, 2 * TK)}
    qpos = {nk: lax.broadcasted_iota(jnp.int32, (nk, TQ), 1) for nk in (TK, 2 * TK)}
    diag_mask = {"single": kpos[TK] <= qpos[TK], "pair": kpos[2 * TK] - TK <= qpos[2 * TK]}
    slot_head = lax.broadcasted_iota(jnp.int32, (LANES, TQ), 0) // BIAS_SLOTS
    qbt = qbt_ref[0, 0]
    for h in range(FOX_HEADS):
        qb_scr[h] = jnp.where(slot_head == h, qbt, jnp.zeros_like(qbt))
    m_scr[...] = jnp.full(m_scr.shape, NEG_INF, F32)
    acc_scr[...] = jnp.zeros_like(acc_scr)
    ones_rows = {nk: jnp.ones((BF16_ROWS, nk), BF16) for nk in (TK, 2 * TK)}

    n_keys = {"pair": 2 * TK, "single": TK}
    pair = lambda u: (u, "pair")

    def q_op(unit, par, h):
        idx, kind = unit
        nk = n_keys[kind]
        c0 = h * FOX_HEAD_DIM
        k0 = pl.multiple_of(idx * nk, nk)
        ka = jnp.concatenate([k_ref[0, pl.ds(k0, nk), c0:c0 + FOX_HEAD_DIM],
                              kb_ref[0, pl.ds(k0, nk), :]], axis=1)
        qa = jnp.concatenate([qt_ref[0, 0, c0:c0 + FOX_HEAD_DIM, :], qb_scr[h]], axis=0)
        st = _dot(ka, qa)
        s_scr[par, h, :nk, :] = st
        ml_scr[par, h, 0:1, :] = jnp.max(st, axis=0, keepdims=True)

    def xv_op(unit, par, h, masked):
        idx, kind = unit
        nk = n_keys[kind]
        c0 = h * FOX_HEAD_DIM
        st = s_scr[par, h, :nk, :]
        if masked:
            st = jnp.where(diag_mask[kind], st, NEG_INF)
            m_loc = jnp.max(st, axis=0, keepdims=True)
        else:
            m_loc = ml_scr[par, h, 0:1, :]
        m_old = m_scr[h, 0:1, :]
        m_new = jnp.maximum(m_old, m_loc)
        alpha = jnp.exp2(m_old - m_new)
        pt = jnp.exp2(st - m_new).astype(BF16)
        m_scr[h, 0:1, :] = m_new
        chunk = idx if kind == "pair" else idx // 2
        half_rows = D_MODEL // 2
        vt_half, r0 = (vt_lo_ref, c0) if c0 < half_rows else (vt_hi_ref, c0 - half_rows)
        va = jnp.concatenate([vt_half[0, chunk, r0:r0 + FOX_HEAD_DIM, :nk], ones_rows[nk]],
                             axis=0)
        acc_scr[h] = alpha * acc_scr[h] + _dot(va, pt)

    def slot(q=None, xv=None):
        for h in range(FOX_HEADS):
            if xv is not None:
                xv_op(xv[0], xv[1], h, xv[2])
            if q is not None:
                q_op(q[0], q[1], h)

    n_full = qi // 2

    @pl.when(n_full >= 1)
    def _():
        slot(q=(pair(0), 0))

    @pl.loop(0, jnp.maximum(n_full - 1, 0) // 2)
    def _(i):
        t = 2 * i
        slot(q=(pair(t + 1), 1), xv=(pair(t), 0, False))
        slot(q=(pair(t + 2), 0), xv=(pair(t + 1), 1, False))

    def output_stage():
        for h in range(FOX_HEADS):
            c0 = h * FOX_HEAD_DIM
            acc = acc_scr[h]
            ot = acc[:FOX_HEAD_DIM] / acc[FOX_HEAD_DIM:FOX_HEAD_DIM + 1]
            o = ot.T * zb_ref[0, :, c0:c0 + FOX_HEAD_DIM].astype(F32)
            o_scr[:, c0:c0 + FOX_HEAD_DIM] = o.astype(BF16)
        pb = _dot(o_scr[...], wpb_ref[...])
        merged = gb_ref[0].astype(F32) * pb + ma_ref[0].astype(F32)
        hres = x_ref[0] + _dot(merged.astype(BF16), wout_ref[...])
        ms = jnp.mean(hres * hres, axis=-1, keepdims=True)
        out_ref[0] = hres * lax.rsqrt(ms + EPS) * gf_ref[...]

    odd_blocks = qi % 2 == 0
    for last, is_kind in (((qi, "single"), odd_blocks),
                          ((n_full, "pair"), jnp.logical_not(odd_blocks))):
        @pl.when(is_kind & (n_full == 0))
        def _():
            slot(q=(last, 0))
            slot(xv=(last, 0, True))
            output_stage()

        @pl.when(is_kind & (n_full % 2 == 1))
        def _():
            slot(q=(last, 1), xv=(pair(n_full - 1), 0, False))
            slot(xv=(last, 1, True))
            output_stage()

        @pl.when(is_kind & (n_full >= 2) & (n_full % 2 == 0))
        def _():
            slot(q=(pair(n_full - 1), 1), xv=(pair(n_full - 2), 0, False))
            slot(q=(last, 0), xv=(pair(n_full - 1), 1, False))
            slot(xv=(last, 0, True))
            output_stage()


def _resident(shape):
    return pl.BlockSpec(shape, lambda b, s: (0,) * len(shape), pipeline_mode=pl.Buffered(1))


def _bias_slot_constants():
    pk = np.zeros((LANES, LANES), np.float32)
    pq = np.zeros((LANES, LANES), np.float32)
    kones = np.zeros((1, LANES), np.float32)
    qones = np.zeros((LANES, TS), np.float32)
    for h in range(FOX_HEADS):
        for i in range(3):
            pk[FOX_HEADS * i + h, BIAS_SLOTS * h + i] = -1.0
            qones[BIAS_SLOTS * h + i, :] = 1.0
            pq[BIAS_SLOTS * h + 3 + i, FOX_HEADS * i + h] = 1.0
            kones[0, BIAS_SLOTS * h + 3 + i] = 1.0
    return (jnp.asarray(pk, BF16), jnp.asarray(kones), jnp.asarray(pq, BF16), jnp.asarray(qones))


def kernel(x, norm1_g, w_in, sgu_ln_g, sgu_ln_b, w_spatial, b_spatial, b_forget, b_gate,
           w_proj_a, w_proj_b, w_out, norm_f_g):
    B, S, D = x.shape
    assert D == D_MODEL and S % TS == 0 and TQ == TK and KEY_BLOCKS_PER_CHUNK == 2
    H = FOX_HEADS

    w_bf = w_in.astype(BF16)
    w_cols = lambda width, idx: pl.BlockSpec((D, width), lambda b, s: (0, idx),
                                             pipeline_mode=pl.Buffered(1))
    w_f = jnp.pad(w_bf[:, 7 * D:7 * D + H], ((0, 0), (0, BF16_ROWS - H)))
    w_qvf = jnp.concatenate([w_bf[:, 3 * D:4 * D], w_bf[:, 5 * D:6 * D], w_f], axis=1)
    w_g = w_bf[:, 7 * D + H:]
    b_ft = jnp.broadcast_to(b_forget[:, None], (H, TS))
    row = lambda t: t.reshape(1, -1)
    pk, kones, pq, qones = _bias_slot_constants()

    NS = S // TS
    tok = lambda: pl.BlockSpec((1, TS, D), lambda b, s: (b, s, 0))
    feat = lambda rows: pl.BlockSpec((1, 1, rows, TS), lambda b, s: (b, s, 0, 0))
    tok_out = jax.ShapeDtypeStruct((B, S, D), BF16)
    feat_out = lambda rows: jax.ShapeDtypeStruct((B, NS, rows, TS), BF16)
    qt, qbt, k, kb, vt, zb, gb, ma = pl.pallas_call(
        _proj_kernel,
        grid=(B, NS),
        in_specs=[
            tok(),
            _resident((1, D)), w_cols(3 * D, 0), w_cols(D, 4), w_cols(D, 6),
            _resident((D, 2 * D + BF16_ROWS)), _resident((D, 2 * D)),
            _resident((1, D)), _resident((1, D)),
            _resident((A_GROUPS, SGU_BLOCK, SGU_BLOCK)), _resident((SGU_BLOCK, A_GROUPS)),
            _resident((H, TS)), _resident((1, 2 * D)), _resident((D, D)),
            _resident((LANES, LANES)), _resident((1, LANES)), _resident((LANES, LANES)),
            _resident((LANES, TS)),
        ],
        out_specs=[feat(D), feat(LANES), tok(),
                   pl.BlockSpec((1, TS, LANES), lambda b, s: (b, s, 0)),
                   feat(D), tok(), tok(), tok()],
        out_shape=[feat_out(D), feat_out(LANES), tok_out,
                   jax.ShapeDtypeStruct((B, S, LANES), BF16),
                   feat_out(D), tok_out, tok_out, tok_out],
        scratch_shapes=[pltpu.VMEM((TS, D), BF16), pltpu.VMEM((H, TS), F32)],
        compiler_params=pltpu.CompilerParams(
            dimension_semantics=("arbitrary", "arbitrary"), vmem_limit_bytes=VMEM_LIMIT_BYTES),
        name="proj_sgu",
    )(x, row(norm1_g), w_bf, w_bf, w_bf, w_qvf, w_g,
      row(sgu_ln_g), row(sgu_ln_b), w_spatial,
      b_spatial.T, b_ft, b_gate.reshape(1, 2 * D), w_proj_a.astype(BF16), pk, kones, pq, qones)

    qtok = lambda: pl.BlockSpec((1, TQ, D), lambda b, s: (b, s, 0))
    qfeat = lambda rows: pl.BlockSpec(
        (1, 1, rows, TQ),
        lambda b, s: (b, s // KEY_BLOCKS_PER_CHUNK, 0, s % KEY_BLOCKS_PER_CHUNK))
    per_batch = lambda shape, **kw: pl.BlockSpec((1,) + shape,
                                                 lambda b, s: (b,) + (0,) * len(shape), **kw)
    vt_half = lambda half, **kw: pl.BlockSpec((1, NS, D // 2, TS), lambda b, s: (b, 0, half, 0),
                                              **kw)
    return pl.pallas_call(
        _attn_kernel,
        grid=(B, S // TQ),
        in_specs=[
            qfeat(D), qfeat(LANES),
            per_batch((S, D)), per_batch((S, LANES), pipeline_mode=pl.Buffered(1)),
            vt_half(0), vt_half(1, pipeline_mode=pl.Buffered(1)),
            qtok(), qtok(), qtok(), qtok(),
            _resident((D, D)), _resident((D, D)), _resident((1, D)),
        ],
        out_specs=qtok(),
        out_shape=jax.ShapeDtypeStruct((B, S, D), x.dtype),
        scratch_shapes=[pltpu.VMEM((TQ, D), BF16),
                        pltpu.VMEM((H, LANES, TQ), BF16),
                        pltpu.VMEM((H, 8, TQ), F32),
                        pltpu.VMEM((H, ACC_ROWS, TQ), F32),
                        pltpu.VMEM((2, H, 2 * TK, TQ), F32), pltpu.VMEM((2, H, 8, TQ), F32)],
        compiler_params=pltpu.CompilerParams(
            dimension_semantics=("arbitrary", "arbitrary"), vmem_limit_bytes=VMEM_LIMIT_BYTES),
        name="fox_merge",
    )(qt, qbt, k, kb, vt, vt, zb, gb, ma, x, w_proj_b.astype(BF16), w_out.astype(BF16),
      row(norm_f_g))
```

```python
import math

import numpy as np
import jax
import jax.numpy as jnp
from jax import lax
from jax.experimental import pallas as pl
from jax.experimental.pallas import tpu as pltpu

D_MODEL = 1024
CHUNK = 64
SGU_BLOCK = 128
A_GROUPS = 8
A_GROUP_DIM = D_MODEL // A_GROUPS
GATE_CHUNK = 2 * D_MODEL // A_GROUPS
FOX_HEADS = 8
FOX_HEAD_DIM = D_MODEL // FOX_HEADS
EPS = 1e-6
NEG_INF = -1e30
LOG2E = math.log2(math.e)

LANES = 128
BF16_ROWS = 16
TS = 512
TQ = 256
TK = 256
KEY_BLOCKS_PER_CHUNK = TS // TK
VMEM_LIMIT_BYTES = 60000 * 1024
BIAS_SLOTS = LANES // FOX_HEADS
ACC_ROWS = FOX_HEAD_DIM + BF16_ROWS

F32 = jnp.float32
BF16 = jnp.bfloat16


def _dot(a, b):
    return jnp.dot(a, b, preferred_element_type=F32)


def _dot_tn(w, t):
    return lax.dot_general(w, t, (((0,), (1,)), ((), ())), preferred_element_type=F32)


def _dot_tt(a, b):
    return lax.dot_general(a, b, (((0,), (0,)), ((), ())), preferred_element_type=F32)


def _sigmoid(t):
    return 1.0 / (1.0 + jnp.exp(-t))


def _split3_rows(t):
    hi = t.astype(BF16).astype(F32)
    r1 = t - hi
    mid = r1.astype(BF16).astype(F32)
    lo = (r1 - mid).astype(BF16).astype(F32)
    pad = jnp.zeros((LANES - 3 * t.shape[0], t.shape[1]), F32)
    return jnp.concatenate([hi, mid, lo, pad], axis=0).astype(BF16)


def _proj_kernel(x_ref, g1_ref, wsgu_ref, wk_ref, wzb_ref, wqvf_ref, wg_ref,
                 lng_ref, lnb_ref,
                 ws_ref, bst_ref, bft_ref, bg_ref, wpa_ref, pk_ref, kones_ref, pq_ref, qones_ref,
                 qt_ref, qbt_ref, k_ref, kb_ref, vt_ref, zb_ref, gb_ref, ma_ref,
                 a_scr, carry_scr, sga_scr):
    @pl.when(pl.program_id(1) == 0)
    def _():
        carry_scr[...] = jnp.zeros_like(carry_scr)

    x = x_ref[0]
    ms = jnp.mean(x * x, axis=-1, keepdims=True)
    xn = (x * lax.rsqrt(ms + EPS) * g1_ref[...]).astype(BF16)

    uvz = _dot(xn, wsgu_ref[...])
    pos = lax.broadcasted_iota(jnp.int32, (SGU_BLOCK, SGU_BLOCK), 0) // CHUNK
    src = lax.broadcasted_iota(jnp.int32, (SGU_BLOCK, SGU_BLOCK), 1) // CHUNK
    causal = pos >= src
    for g in range(A_GROUPS):
        c0 = g * A_GROUP_DIM
        ws = jnp.where(causal, ws_ref[g], 0.0).astype(BF16)
        bias = bst_ref[:, g:g + 1]
        lng = lng_ref[:, c0:c0 + A_GROUP_DIM]
        lnb = lnb_ref[:, c0:c0 + A_GROUP_DIM]
        vns = []
        for n in range(TS // SGU_BLOCK):
            r0 = n * SGU_BLOCK
            v = uvz[r0:r0 + SGU_BLOCK, D_MODEL + c0:D_MODEL + c0 + A_GROUP_DIM]
            mu = jnp.mean(v, axis=-1, keepdims=True)
            d = v - mu
            var = jnp.mean(d * d, axis=-1, keepdims=True)
            vns.append(((d * lax.rsqrt(var + EPS)) * lng + lnb).astype(BF16))
        y_all = _dot(ws, jnp.concatenate(vns, axis=1))
        for n in range(TS // SGU_BLOCK):
            r0 = n * SGU_BLOCK
            u = uvz[r0:r0 + SGU_BLOCK, c0:c0 + A_GROUP_DIM]
            z = uvz[r0:r0 + SGU_BLOCK, 2 * D_MODEL + c0:2 * D_MODEL + c0 + A_GROUP_DIM]
            y = y_all[:, n * A_GROUP_DIM:(n + 1) * A_GROUP_DIM] + bias
            a = u * y * (z * _sigmoid(z))
            a_scr[r0:r0 + SGU_BLOCK, c0:c0 + A_GROUP_DIM] = a.astype(BF16)
        g0 = g * GATE_CHUNK
        gate = _sigmoid(_dot(xn, wg_ref[:, g0:g0 + GATE_CHUNK]) + bg_ref[:, g0:g0 + GATE_CHUNK])
        if g0 < D_MODEL:
            sga_scr[:, g0:g0 + GATE_CHUNK] = gate
        else:
            gb_ref[0, :, g0 - D_MODEL:g0 - D_MODEL + GATE_CHUNK] = gate.astype(BF16)

    ma = sga_scr[...] * _dot(a_scr[...], wpa_ref[...])
    ma_ref[0] = ma.astype(BF16)

    k_ref[0] = _dot(xn, wk_ref[...]).astype(BF16)
    zb = _dot(xn, wzb_ref[...])
    zb_ref[0] = (zb * _sigmoid(zb)).astype(BF16)
    qvft = _dot_tn(wqvf_ref[...], xn)
    qt_ref[0, 0] = (qvft[:D_MODEL] * (LOG2E / math.sqrt(FOX_HEAD_DIM))).astype(BF16)
    vt_ref[0, 0] = qvft[D_MODEL:2 * D_MODEL].astype(BF16)

    f = qvft[2 * D_MODEL:2 * D_MODEL + FOX_HEADS] + bft_ref[...]
    logf = jnp.minimum(f, 0.0) - jnp.log1p(jnp.exp(-jnp.abs(f)))
    triu = (lax.broadcasted_iota(jnp.int32, (TS, TS), 0)
            <= lax.broadcasted_iota(jnp.int32, (TS, TS), 1)).astype(BF16)
    r = _dot(_split3_rows(logf), triu)
    c = carry_scr[...] + (r[0:FOX_HEADS] + r[FOX_HEADS:2 * FOX_HEADS]
                          + r[2 * FOX_HEADS:3 * FOX_HEADS])
    carry_scr[...] = jnp.broadcast_to(c[:, TS - 1:TS], c.shape)
    parts = _split3_rows(c * LOG2E)
    qb = qones_ref[...] + _dot(pq_ref[...], parts)
    qbt_ref[0, 0] = qb.astype(BF16)
    kb = kones_ref[...] + _dot_tt(parts, pk_ref[...])
    kb_ref[0] = kb.astype(BF16)


def _attn_kernel(qt_ref, qbt_ref, k_ref, kb_ref, vt_lo_ref, vt_hi_ref, zb_ref, gb_ref, ma_ref,
                 x_ref,
                 wpb_ref, wout_ref, gf_ref, out_ref,
                 o_scr, qb_scr, m_scr, acc_scr, s_scr, ml_scr):
    qi = pl.program_id(1)
    kpos = {nk: lax.broadcasted_iota(jnp.int32, (nk, TQ), 0) for nk in (TK, 2 * TK)}
    qpos = {nk: lax.broadcasted_iota(jnp.int32, (nk, TQ), 1) for nk in (TK, 2 * TK)}
    diag_mask = {"single": kpos[TK] <= qpos[TK], "pair": kpos[2 * TK] - TK <= qpos[2 * TK]}
    slot_head = lax.broadcasted_iota(jnp.int32, (LANES, TQ), 0) // BIAS_SLOTS
    qbt = qbt_ref[0, 0]
    for h in range(FOX_HEADS):
        qb_scr[h] = jnp.where(slot_head == h, qbt, jnp.zeros_like(qbt))
    m_scr[...] = jnp.full(m_scr.shape, NEG_INF, F32)
    acc_scr[...] = jnp.zeros_like(acc_scr)
    ones_rows = {nk: jnp.ones((BF16_ROWS, nk), BF16) for nk in (TK, 2 * TK)}

    n_keys = {"pair": 2 * TK, "single": TK}
    pair = lambda u: (u, "pair")

    def q_op(unit, par, h):
        idx, kind = unit
        nk = n_keys[kind]
        c0 = h * FOX_HEAD_DIM
        k0 = pl.multiple_of(idx * nk, nk)
        ka = jnp.concatenate([k_ref[0, pl.ds(k0, nk), c0:c0 + FOX_HEAD_DIM],
                              kb_ref[0, pl.ds(k0, nk), :]], axis=1)
        qa = jnp.concatenate([qt_ref[0, 0, c0:c0 + FOX_HEAD_DIM, :], qb_scr[h]], axis=0)
        st = _dot(ka, qa)
        s_scr[par, h, :nk, :] = st
        ml_scr[par, h, 0:1, :] = jnp.max(st, axis=0, keepdims=True)

    def xv_op(unit, par, h, masked):
        idx, kind = unit
        nk = n_keys[kind]
        c0 = h * FOX_HEAD_DIM
        st = s_scr[par, h, :nk, :]
        if masked:
            st = jnp.where(diag_mask[kind], st, NEG_INF)
            m_loc = jnp.max(st, axis=0, keepdims=True)
        else:
            m_loc = ml_scr[par, h, 0:1, :]
        m_old = m_scr[h, 0:1, :]
        m_new = jnp.maximum(m_old, m_loc)
        alpha = jnp.exp2(m_old - m_new)
        pt = jnp.exp2(st - m_new).astype(BF16)
        m_scr[h, 0:1, :] = m_new
        chunk = idx if kind == "pair" else idx // 2
        half_rows = D_MODEL // 2
        vt_half, r0 = (vt_lo_ref, c0) if c0 < half_rows else (vt_hi_ref, c0 - half_rows)
        va = jnp.concatenate([vt_half[0, chunk, r0:r0 + FOX_HEAD_DIM, :nk], ones_rows[nk]],
                             axis=0)
        acc_scr[h] = alpha * acc_scr[h] + _dot(va, pt)

    def slot(q=None, xv=None):
        for h in range(FOX_HEADS):
            if xv is not None:
                xv_op(xv[0], xv[1], h, xv[2])
            if q is not None:
                q_op(q[0], q[1], h)

    n_full = qi // 2

    @pl.when(n_full >= 1)
    def _():
        slot(q=(pair(0), 0))

    @pl.loop(0, jnp.maximum(n_full - 1, 0) // 2)
    def _(i):
        t = 2 * i
        slot(q=(pair(t + 1), 1), xv=(pair(t), 0, False))
        slot(q=(pair(t + 2), 0), xv=(pair(t + 1), 1, False))

    def output_stage():
        for h in range(FOX_HEADS):
            c0 = h * FOX_HEAD_DIM
            acc = acc_scr[h]
            ot = acc[:FOX_HEAD_DIM] / acc[FOX_HEAD_DIM:FOX_HEAD_DIM + 1]
            o = ot.T * zb_ref[0, :, c0:c0 + FOX_HEAD_DIM].astype(F32)
            o_scr[:, c0:c0 + FOX_HEAD_DIM] = o.astype(BF16)
        pb = _dot(o_scr[...], wpb_ref[...])
        merged = gb_ref[0].astype(F32) * pb + ma_ref[0].astype(F32)
        hres = x_ref[0] + _dot(merged.astype(BF16), wout_ref[...])
        ms = jnp.mean(hres * hres, axis=-1, keepdims=True)
        out_ref[0] = hres * lax.rsqrt(ms + EPS) * gf_ref[...]

    odd_blocks = qi % 2 == 0
    for last, is_kind in (((qi, "single"), odd_blocks),
                          ((n_full, "pair"), jnp.logical_not(odd_blocks))):
        @pl.when(is_kind & (n_full == 0))
        def _():
            slot(q=(last, 0))
            slot(xv=(last, 0, True))
            output_stage()

        @pl.when(is_kind & (n_full % 2 == 1))
        def _():
            slot(q=(last, 1), xv=(pair(n_full - 1), 0, False))
            slot(xv=(last, 1, True))
            output_stage()

        @pl.when(is_kind & (n_full >= 2) & (n_full % 2 == 0))
        def _():
            slot(q=(pair(n_full - 1), 1), xv=(pair(n_full - 2), 0, False))
            slot(q=(last, 0), xv=(pair(n_full - 1), 1, False))
            slot(xv=(last, 0, True))
            output_stage()


def _resident(shape):
    return pl.BlockSpec(shape, lambda b, s: (0,) * len(shape), pipeline_mode=pl.Buffered(1))


def _bias_slot_constants():
    pk = np.zeros((LANES, LANES), np.float32)
    pq = np.zeros((LANES, LANES), np.float32)
    kones = np.zeros((1, LANES), np.float32)
    qones = np.zeros((LANES, TS), np.float32)
    for h in range(FOX_HEADS):
        for i in range(3):
            pk[FOX_HEADS * i + h, BIAS_SLOTS * h + i] = -1.0
            qones[BIAS_SLOTS * h + i, :] = 1.0
            pq[BIAS_SLOTS * h + 3 + i, FOX_HEADS * i + h] = 1.0
            kones[0, BIAS_SLOTS * h + 3 + i] = 1.0
    return (jnp.asarray(pk, BF16), jnp.asarray(kones), jnp.asarray(pq, BF16), jnp.asarray(qones))


def kernel(x, norm1_g, w_in, sgu_ln_g, sgu_ln_b, w_spatial, b_spatial, b_forget, b_gate,
           w_proj_a, w_proj_b, w_out, norm_f_g):
    B, S, D = x.shape
    assert D == D_MODEL and S % TS == 0 and TQ == TK and KEY_BLOCKS_PER_CHUNK == 2
    H = FOX_HEADS

    w_bf = w_in.astype(BF16)
    w_cols = lambda width, idx: pl.BlockSpec((D, width), lambda b, s: (0, idx),
                                             pipeline_mode=pl.Buffered(1))
    w_f = jnp.pad(w_bf[:, 7 * D:7 * D + H], ((0, 0), (0, BF16_ROWS - H)))
    w_qvf = jnp.concatenate([w_bf[:, 3 * D:4 * D], w_bf[:, 5 * D:6 * D], w_f], axis=1)
    w_g = w_bf[:, 7 * D + H:]
    b_ft = jnp.broadcast_to(b_forget[:, None], (H, TS))
    row = lambda t: t.reshape(1, -1)
    pk, kones, pq, qones = _bias_slot_constants()

    NS = S // TS
    tok = lambda: pl.BlockSpec((1, TS, D), lambda b, s: (b, s, 0))
    feat = lambda rows: pl.BlockSpec((1, 1, rows, TS), lambda b, s: (b, s, 0, 0))
    tok_out = jax.ShapeDtypeStruct((B, S, D), BF16)
    feat_out = lambda rows: jax.ShapeDtypeStruct((B, NS, rows, TS), BF16)
    qt, qbt, k, kb, vt, zb, gb, ma = pl.pallas_call(
        _proj_kernel,
        grid=(B, NS),
        in_specs=[
            tok(),
            _resident((1, D)), w_cols(3 * D, 0), w_cols(D, 4), w_cols(D, 6),
            _resident((D, 2 * D + BF16_ROWS)), _resident((D, 2 * D)),
            _resident((1, D)), _resident((1, D)),
            _resident((A_GROUPS, SGU_BLOCK, SGU_BLOCK)), _resident((SGU_BLOCK, A_GROUPS)),
            _resident((H, TS)), _resident((1, 2 * D)), _resident((D, D)),
            _resident((LANES, LANES)), _resident((1, LANES)), _resident((LANES, LANES)),
            _resident((LANES, TS)),
        ],
        out_specs=[feat(D), feat(LANES), tok(),
                   pl.BlockSpec((1, TS, LANES), lambda b, s: (b, s, 0)),
                   feat(D), tok(), tok(), tok()],
        out_shape=[feat_out(D), feat_out(LANES), tok_out,
                   jax.ShapeDtypeStruct((B, S, LANES), BF16),
                   feat_out(D), tok_out, tok_out, tok_out],
        scratch_shapes=[pltpu.VMEM((TS, D), BF16), pltpu.VMEM((H, TS), F32),
                        pltpu.VMEM((TS, D), F32)],
        compiler_params=pltpu.CompilerParams(
            dimension_semantics=("arbitrary", "arbitrary"), vmem_limit_bytes=VMEM_LIMIT_BYTES),
        name="proj_sgu",
    )(x, row(norm1_g), w_bf, w_bf, w_bf, w_qvf, w_g,
      row(sgu_ln_g), row(sgu_ln_b), w_spatial,
      b_spatial.T, b_ft, b_gate.reshape(1, 2 * D), w_proj_a.astype(BF16), pk, kones, pq, qones)

    qtok = lambda: pl.BlockSpec((1, TQ, D), lambda b, s: (b, s, 0))
    qfeat = lambda rows: pl.BlockSpec(
        (1, 1, rows, TQ),
        lambda b, s: (b, s // KEY_BLOCKS_PER_CHUNK, 0, s % KEY_BLOCKS_PER_CHUNK))
    per_batch = lambda shape, **kw: pl.BlockSpec((1,) + shape,
                                                 lambda b, s: (b,) + (0,) * len(shape), **kw)
    vt_half = lambda half, **kw: pl.BlockSpec((1, NS, D // 2, TS), lambda b, s: (b, 0, half, 0),
                                              **kw)
    return pl.pallas_call(
        _attn_kernel,
        grid=(B, S // TQ),
        in_specs=[
            qfeat(D), qfeat(LANES),
            per_batch((S, D)), per_batch((S, LANES), pipeline_mode=pl.Buffered(1)),
            vt_half(0), vt_half(1, pipeline_mode=pl.Buffered(1)),
            qtok(), qtok(), qtok(), qtok(),
            _resident((D, D)), _resident((D, D)), _resident((1, D)),
        ],
        out_specs=qtok(),
        out_shape=jax.ShapeDtypeStruct((B, S, D), x.dtype),
        scratch_shapes=[pltpu.VMEM((TQ, D), BF16),
                        pltpu.VMEM((H, LANES, TQ), BF16),
                        pltpu.VMEM((H, 8, TQ), F32),
                        pltpu.VMEM((H, ACC_ROWS, TQ), F32),
                        pltpu.VMEM((2, H, 2 * TK, TQ), F32), pltpu.VMEM((2, H, 8, TQ), F32)],
        compiler_params=pltpu.CompilerParams(
            dimension_semantics=("arbitrary", "arbitrary"), vmem_limit_bytes=VMEM_LIMIT_BYTES),
        name="fox_merge",
    )(qt, qbt, k, kb, vt, vt, zb, gb, ma, x, w_proj_b.astype(BF16), w_out.astype(BF16),
      row(norm_f_g))
```

```python
import math

import numpy as np
import jax
import jax.numpy as jnp
from jax import lax
from jax.experimental import pallas as pl
from jax.experimental.pallas import tpu as pltpu

D_MODEL = 1024
CHUNK = 64
SGU_BLOCK = 128
A_GROUPS = 8
A_GROUP_DIM = D_MODEL // A_GROUPS
GATE_CHUNK = 2 * D_MODEL // A_GROUPS
FOX_HEADS = 8
FOX_HEAD_DIM = D_MODEL // FOX_HEADS
EPS = 1e-6
NEG_INF = -1e30
LOG2E = math.log2(math.e)

LANES = 128
BF16_ROWS = 16
TS = 512
TQ = 256
TK = 256
KEY_BLOCKS_PER_CHUNK = TS // TK
VMEM_LIMIT_BYTES = 60000 * 1024
BIAS_SLOTS = LANES // FOX_HEADS
ACC_ROWS = FOX_HEAD_DIM + BF16_ROWS

F32 = jnp.float32
BF16 = jnp.bfloat16


def _dot(a, b):
    return jnp.dot(a, b, preferred_element_type=F32)


def _dot_tn(w, t):
    return lax.dot_general(w, t, (((0,), (1,)), ((), ())), preferred_element_type=F32)


def _dot_tt(a, b):
    return lax.dot_general(a, b, (((0,), (0,)), ((), ())), preferred_element_type=F32)


def _sigmoid(t):
    return 1.0 / (1.0 + jnp.exp(-t))


def _split3_rows(t):
    hi = t.astype(BF16).astype(F32)
    r1 = t - hi
    mid = r1.astype(BF16).astype(F32)
    lo = (r1 - mid).astype(BF16).astype(F32)
    pad = jnp.zeros((LANES - 3 * t.shape[0], t.shape[1]), F32)
    return jnp.concatenate([hi, mid, lo, pad], axis=0).astype(BF16)


def _proj_kernel(x_ref, g1_ref, wsgu_ref, wk_ref, wzb_ref, wqvf_ref, wg_ref,
                 lng_ref, lnb_ref,
                 ws_ref, bst_ref, bft_ref, bg_ref, wpa_ref, pk_ref, kones_ref, pq_ref, qones_ref,
                 qt_ref, qbt_ref, k_ref, kb_ref, vt_ref, zb_ref, gb_ref, ma_ref,
                 a_scr, carry_scr, sga_scr):
    @pl.when(pl.program_id(1) == 0)
    def _():
        carry_scr[...] = jnp.zeros_like(carry_scr)

    x = x_ref[0]
    ms = jnp.mean(x * x, axis=-1, keepdims=True)
    xn = (x * lax.rsqrt(ms + EPS) * g1_ref[...]).astype(BF16)

    uvz = _dot(xn, wsgu_ref[...])
    pos = lax.broadcasted_iota(jnp.int32, (SGU_BLOCK, SGU_BLOCK), 0) // CHUNK
    src = lax.broadcasted_iota(jnp.int32, (SGU_BLOCK, SGU_BLOCK), 1) // CHUNK
    causal = pos >= src
    for g in range(A_GROUPS):
        c0 = g * A_GROUP_DIM
        ws = jnp.where(causal, ws_ref[g], 0.0).astype(BF16)
        bias = bst_ref[:, g:g + 1]
        lng = lng_ref[:, c0:c0 + A_GROUP_DIM]
        lnb = lnb_ref[:, c0:c0 + A_GROUP_DIM]
        vns = []
        for n in range(TS // SGU_BLOCK):
            r0 = n * SGU_BLOCK
            v = uvz[r0:r0 + SGU_BLOCK, D_MODEL + c0:D_MODEL + c0 + A_GROUP_DIM]
            mu = jnp.mean(v, axis=-1, keepdims=True)
            d = v - mu
            var = jnp.mean(d * d, axis=-1, keepdims=True)
            vns.append(((d * lax.rsqrt(var + EPS)) * lng + lnb).astype(BF16))
        y_all = _dot(ws, jnp.concatenate(vns, axis=1))
        for n in range(TS // SGU_BLOCK):
            r0 = n * SGU_BLOCK
            u = uvz[r0:r0 + SGU_BLOCK, c0:c0 + A_GROUP_DIM]
            z = uvz[r0:r0 + SGU_BLOCK, 2 * D_MODEL + c0:2 * D_MODEL + c0 + A_GROUP_DIM]
            y = y_all[:, n * A_GROUP_DIM:(n + 1) * A_GROUP_DIM] + bias
            a = u * y * (z * _sigmoid(z))
            a_scr[r0:r0 + SGU_BLOCK, c0:c0 + A_GROUP_DIM] = a.astype(BF16)
        g0 = g * GATE_CHUNK
        gate = _sigmoid(_dot(xn, wg_ref[:, g0:g0 + GATE_CHUNK]) + bg_ref[:, g0:g0 + GATE_CHUNK])
        if g0 < D_MODEL:
            sga_scr[:, g0:g0 + GATE_CHUNK] = gate
        else:
            gb_ref[0, :, g0 - D_MODEL:g0 - D_MODEL + GATE_CHUNK] = gate.astype(BF16)

    ma = sga_scr[...] * _dot(a_scr[...], wpa_ref[...])
    ma_ref[0] = ma.astype(BF16)

    k_ref[0] = _dot(xn, wk_ref[...]).astype(BF16)
    zb = _dot(xn, wzb_ref[...])
    zb_ref[0] = (zb * _sigmoid(zb)).astype(BF16)
    qvft = _dot_tn(wqvf_ref[...], xn)
    qt_ref[0, 0] = (qvft[:D_MODEL] * (LOG2E / math.sqrt(FOX_HEAD_DIM))).astype(BF16)
    vt_ref[0, 0] = qvft[D_MODEL:2 * D_MODEL].astype(BF16)

    f = qvft[2 * D_MODEL:2 * D_MODEL + FOX_HEADS] + bft_ref[...]
    logf = jnp.minimum(f, 0.0) - jnp.log1p(jnp.exp(-jnp.abs(f)))
    triu = (lax.broadcasted_iota(jnp.int32, (TS, TS), 0)
            <= lax.broadcasted_iota(jnp.int32, (TS, TS), 1)).astype(BF16)
    r = _dot(_split3_rows(logf), triu)
    c = carry_scr[...] + (r[0:FOX_HEADS] + r[FOX_HEADS:2 * FOX_HEADS]
                          + r[2 * FOX_HEADS:3 * FOX_HEADS])
    carry_scr[...] = jnp.broadcast_to(c[:, TS - 1:TS], c.shape)
    parts = _split3_rows(c * LOG2E)
    qb = qones_ref[...] + _dot(pq_ref[...], parts)
    qbt_ref[0, 0] = qb.astype(BF16)
    kb = kones_ref[...] + _dot_tt(parts, pk_ref[...])
    kb_ref[0] = kb.astype(BF16)


def _attn_kernel(qt_ref, qbt_ref, k_ref, kb_ref, vt_lo_ref, vt_hi_ref, zb_ref, gb_ref, ma_ref,
                 x_ref,
                 wpb_ref, wout_ref, gf_ref, out_ref,
                 o_scr, qb_scr, m_scr, acc_scr, s_scr, ml_scr):
    qi = pl.program_id(1)
    kpos = {nk: lax.broadcasted_iota(jnp.int32, (nk, TQ), 0) for nk in (TK, 2 * TK)}
    qpos = {nk: lax.broadcasted_iota(jnp.int32, (nk, TQ), 1) for nk in (TK, 2 * TK)}
    diag_mask = {"single": kpos[TK] <= qpos[TK], "pair": kpos[2 * TK] - TK <= qpos[2 * TK]}
    slot_head = lax.broadcasted_iota(jnp.int32, (LANES, TQ), 0) // BIAS_SLOTS
    qbt = qbt_ref[0, 0]
    for h in range(FOX_HEADS):
        qb_scr[h] = jnp.where(slot_head == h, qbt, jnp.zeros_like(qbt))
    m_scr[...] = jnp.full(m_scr.shape, NEG_INF, F32)
    acc_scr[...] = jnp.zeros_like(acc_scr)
    ones_rows = {nk: jnp.ones((BF16_ROWS, nk), BF16) for nk in (TK, 2 * TK)}

    n_keys = {"pair": 2 * TK, "single": TK}
    pair = lambda u: (u, "pair")

    def q_op(unit, par, h):
        idx, kind = unit
        nk = n_keys[kind]
        c0 = h * FOX_HEAD_DIM
        k0 = pl.multiple_of(idx * nk, nk)
        ka = jnp.concatenate([k_ref[0, pl.ds(k0, nk), c0:c0 + FOX_HEAD_DIM],
                              kb_ref[0, pl.ds(k0, nk), :]], axis=1)
        qa = jnp.concatenate([qt_ref[0, 0, c0:c0 + FOX_HEAD_DIM, :], qb_scr[h]], axis=0)
        st = _dot(ka, qa)
        s_scr[h, :nk, :] = st
        ml_scr[h, 0:1, :] = jnp.max(st, axis=0, keepdims=True)

    def xv_op(unit, par, h, masked):
        idx, kind = unit
        nk = n_keys[kind]
        c0 = h * FOX_HEAD_DIM
        st = s_scr[h, :nk, :]
        if masked:
            st = jnp.where(diag_mask[kind], st, NEG_INF)
            m_loc = jnp.max(st, axis=0, keepdims=True)
        else:
            m_loc = ml_scr[h, 0:1, :]
        m_old = m_scr[h, 0:1, :]
        m_new = jnp.maximum(m_old, m_loc)
        alpha = jnp.exp2(m_old - m_new)
        pt = jnp.exp2(st - m_new).astype(BF16)
        m_scr[h, 0:1, :] = m_new
        chunk = idx if kind == "pair" else idx // 2
        half_rows = D_MODEL // 2
        vt_half, r0 = (vt_lo_ref, c0) if c0 < half_rows else (vt_hi_ref, c0 - half_rows)
        va = jnp.concatenate([vt_half[0, chunk, r0:r0 + FOX_HEAD_DIM, :nk], ones_rows[nk]],
                             axis=0)
        acc_scr[h] = alpha * acc_scr[h] + _dot(va, pt)

    def slot(q=None, xv=None):
        for h in range(FOX_HEADS):
            if xv is not None:
                xv_op(xv[0], xv[1], h, xv[2])
            if q is not None:
                q_op(q[0], q[1], h)

    n_full = qi // 2

    @pl.when(n_full >= 1)
    def _():
        slot(q=(pair(0), 0))

    @pl.loop(0, jnp.maximum(n_full - 1, 0) // 2)
    def _(i):
        t = 2 * i
        slot(q=(pair(t + 1), 1), xv=(pair(t), 0, False))
        slot(q=(pair(t + 2), 0), xv=(pair(t + 1), 1, False))

    def output_stage():
        for h in range(FOX_HEADS):
            c0 = h * FOX_HEAD_DIM
            acc = acc_scr[h]
            ot = acc[:FOX_HEAD_DIM] / acc[FOX_HEAD_DIM:FOX_HEAD_DIM + 1]
            o = ot.T * zb_ref[0, :, c0:c0 + FOX_HEAD_DIM].astype(F32)
            o_scr[:, c0:c0 + FOX_HEAD_DIM] = o.astype(BF16)
        pb = _dot(o_scr[...], wpb_ref[...])
        merged = gb_ref[0].astype(F32) * pb + ma_ref[0].astype(F32)
        hres = x_ref[0] + _dot(merged.astype(BF16), wout_ref[...])
        ms = jnp.mean(hres * hres, axis=-1, keepdims=True)
        out_ref[0] = hres * lax.rsqrt(ms + EPS) * gf_ref[...]

    odd_blocks = qi % 2 == 0
    for last, is_kind in (((qi, "single"), odd_blocks),
                          ((n_full, "pair"), jnp.logical_not(odd_blocks))):
        @pl.when(is_kind & (n_full == 0))
        def _():
            slot(q=(last, 0))
            slot(xv=(last, 0, True))
            output_stage()

        @pl.when(is_kind & (n_full % 2 == 1))
        def _():
            slot(q=(last, 1), xv=(pair(n_full - 1), 0, False))
            slot(xv=(last, 1, True))
            output_stage()

        @pl.when(is_kind & (n_full >= 2) & (n_full % 2 == 0))
        def _():
            slot(q=(pair(n_full - 1), 1), xv=(pair(n_full - 2), 0, False))
            slot(q=(last, 0), xv=(pair(n_full - 1), 1, False))
            slot(xv=(last, 0, True))
            output_stage()


def _resident(shape):
    return pl.BlockSpec(shape, lambda b, s: (0,) * len(shape), pipeline_mode=pl.Buffered(1))


def _bias_slot_constants():
    pk = np.zeros((LANES, LANES), np.float32)
    pq = np.zeros((LANES, LANES), np.float32)
    kones = np.zeros((1, LANES), np.float32)
    qones = np.zeros((LANES, TS), np.float32)
    for h in range(FOX_HEADS):
        for i in range(3):
            pk[FOX_HEADS * i + h, BIAS_SLOTS * h + i] = -1.0
            qones[BIAS_SLOTS * h + i, :] = 1.0
            pq[BIAS_SLOTS * h + 3 + i, FOX_HEADS * i + h] = 1.0
            kones[0, BIAS_SLOTS * h + 3 + i] = 1.0
    return (jnp.asarray(pk, BF16), jnp.asarray(kones), jnp.asarray(pq, BF16), jnp.asarray(qones))


def kernel(x, norm1_g, w_in, sgu_ln_g, sgu_ln_b, w_spatial, b_spatial, b_forget, b_gate,
           w_proj_a, w_proj_b, w_out, norm_f_g):
    B, S, D = x.shape
    assert D == D_MODEL and S % TS == 0 and TQ == TK and KEY_BLOCKS_PER_CHUNK == 2
    H = FOX_HEADS

    w_bf = w_in.astype(BF16)
    w_cols = lambda width, idx: pl.BlockSpec((D, width), lambda b, s: (0, idx),
                                             pipeline_mode=pl.Buffered(1))
    w_f = jnp.pad(w_bf[:, 7 * D:7 * D + H], ((0, 0), (0, BF16_ROWS - H)))
    w_qvf = jnp.concatenate([w_bf[:, 3 * D:4 * D], w_bf[:, 5 * D:6 * D], w_f], axis=1)
    w_g = w_bf[:, 7 * D + H:]
    b_ft = jnp.broadcast_to(b_forget[:, None], (H, TS))
    row = lambda t: t.reshape(1, -1)
    pk, kones, pq, qones = _bias_slot_constants()

    NS = S // TS
    tok = lambda: pl.BlockSpec((1, TS, D), lambda b, s: (b, s, 0))
    feat = lambda rows: pl.BlockSpec((1, 1, rows, TS), lambda b, s: (b, s, 0, 0))
    tok_out = jax.ShapeDtypeStruct((B, S, D), BF16)
    feat_out = lambda rows: jax.ShapeDtypeStruct((B, NS, rows, TS), BF16)
    qt, qbt, k, kb, vt, zb, gb, ma = pl.pallas_call(
        _proj_kernel,
        grid=(B, NS),
        in_specs=[
            tok(),
            _resident((1, D)), w_cols(3 * D, 0), w_cols(D, 4), w_cols(D, 6),
            _resident((D, 2 * D + BF16_ROWS)), _resident((D, 2 * D)),
            _resident((1, D)), _resident((1, D)),
            _resident((A_GROUPS, SGU_BLOCK, SGU_BLOCK)), _resident((SGU_BLOCK, A_GROUPS)),
            _resident((H, TS)), _resident((1, 2 * D)), _resident((D, D)),
            _resident((LANES, LANES)), _resident((1, LANES)), _resident((LANES, LANES)),
            _resident((LANES, TS)),
        ],
        out_specs=[feat(D), feat(LANES), tok(),
                   pl.BlockSpec((1, TS, LANES), lambda b, s: (b, s, 0)),
                   feat(D), tok(), tok(), tok()],
        out_shape=[feat_out(D), feat_out(LANES), tok_out,
                   jax.ShapeDtypeStruct((B, S, LANES), BF16),
                   feat_out(D), tok_out, tok_out, tok_out],
        scratch_shapes=[pltpu.VMEM((TS, D), BF16), pltpu.VMEM((H, TS), F32),
                        pltpu.VMEM((TS, D), F32)],
        compiler_params=pltpu.CompilerParams(
            dimension_semantics=("arbitrary", "arbitrary"), vmem_limit_bytes=VMEM_LIMIT_BYTES),
        name="proj_sgu",
    )(x, row(norm1_g), w_bf, w_bf, w_bf, w_qvf, w_g,
      row(sgu_ln_g), row(sgu_ln_b), w_spatial,
      b_spatial.T, b_ft, b_gate.reshape(1, 2 * D), w_proj_a.astype(BF16), pk, kones, pq, qones)

    qtok = lambda: pl.BlockSpec((1, TQ, D), lambda b, s: (b, s, 0))
    qfeat = lambda rows: pl.BlockSpec(
        (1, 1, rows, TQ),
        lambda b, s: (b, s // KEY_BLOCKS_PER_CHUNK, 0, s % KEY_BLOCKS_PER_CHUNK))
    per_batch = lambda shape, **kw: pl.BlockSpec((1,) + shape,
                                                 lambda b, s: (b,) + (0,) * len(shape), **kw)
    vt_half = lambda half, **kw: pl.BlockSpec((1, NS, D // 2, TS), lambda b, s: (b, 0, half, 0),
                                              **kw)
    return pl.pallas_call(
        _attn_kernel,
        grid=(B, S // TQ),
        in_specs=[
            qfeat(D), qfeat(LANES),
            per_batch((S, D)), per_batch((S, LANES), pipeline_mode=pl.Buffered(1)),
            vt_half(0), vt_half(1),
            qtok(), qtok(), qtok(), qtok(),
            _resident((D, D)), _resident((D, D)), _resident((1, D)),
        ],
        out_specs=qtok(),
        out_shape=jax.ShapeDtypeStruct((B, S, D), x.dtype),
        scratch_shapes=[pltpu.VMEM((TQ, D), BF16),
                        pltpu.VMEM((H, LANES, TQ), BF16),
                        pltpu.VMEM((H, 8, TQ), F32),
                        pltpu.VMEM((H, ACC_ROWS, TQ), F32),
                        pltpu.VMEM((H, 2 * TK, TQ), F32), pltpu.VMEM((H, 8, TQ), F32)],
        compiler_params=pltpu.CompilerParams(
            dimension_semantics=("arbitrary", "arbitrary"), vmem_limit_bytes=VMEM_LIMIT_BYTES),
        name="fox_merge",
    )(qt, qbt, k, kb, vt, vt, zb, gb, ma, x, w_proj_b.astype(BF16), w_out.astype(BF16),
      row(norm_f_g))
```
